```python
import jax, jax.numpy as jnp
from jax import lax
import numpy as np

D_MODEL = 1024
BATCH = 1
SEQ = 16384
DEPTH = 1
DEC_BATCH = 2
DEC_SEQ = 16384
PAST_LEN = 128

CHUNK = 64
GLA_HEADS = 4
GLA_DK = 64
GLA_DV = 128
GLA_RANK = 16
GLA_GATE_NORM = 16.0
GDN_HEADS = 4
GDN_DK = 128
GDN_DV = 128
SHORT_CONV = 3
D_FF = 2816
FFN_CONV = 3
EPS = 1e-6

GLA_QK = GLA_HEADS * GLA_DK
GLA_V = GLA_HEADS * GLA_DV
GDN_QK = GDN_HEADS * GDN_DK
GDN_V = GDN_HEADS * GDN_DV
MIX_WIDTH = GLA_V + GDN_V
D_IN = 2 * GLA_QK + 2 * GLA_V + 2 * GLA_RANK + 2 * GDN_QK + 2 * GDN_V + 4 * GDN_HEADS

kernel_name = 'hybrid_gla_gdn_convffn_encoder'


def rmsnorm(x, w):
    xf = x.astype(jnp.float32)
    y = xf * lax.rsqrt(jnp.mean(xf * xf, axis=-1, keepdims=True) + EPS)
    return (y * w.astype(jnp.float32)).astype(x.dtype)


def l2norm(x):
    return x * lax.rsqrt(jnp.sum(x * x, axis=-1, keepdims=True) + EPS)


def dwconv_centred(x, w):
    k, c = w.shape
    pad = (k - 1) // 2
    return lax.conv_general_dilated(
        x, w[:, None, :].astype(x.dtype), window_strides=(1,), padding=[(pad, pad)],
        dimension_numbers=('NWC', 'WIO', 'NWC'), feature_group_count=c)


def heads_to_chunks(x, n_heads):
    b, l, f = x.shape
    return x.reshape(b, l // CHUNK, CHUNK, n_heads, f // n_heads).transpose(1, 0, 3, 2, 4)


def gates_to_chunks(x):
    b, l, h = x.shape
    return x.reshape(b, l // CHUNK, CHUNK, h).transpose(1, 0, 3, 2)


def chunks_to_heads(o):
    n, b, h, c, d = o.shape
    return o.transpose(1, 0, 3, 2, 4).reshape(b, n * c, h * d)


def flip(t):
    return jnp.flip(t, axis=1)


def gla_scan(q, k, v, g):
    _, b, h, c, dk = q.shape
    dv = v.shape[-1]
    causal = jnp.tril(jnp.ones((c, c), bool))

    def step(s, inp):
        qc, kc, vc, gc = inp
        cum = jnp.cumsum(gc, axis=-2)
        last = cum[..., -1:, :]
        o_inter = jnp.einsum('bhcd,bhde->bhce', qc * jnp.exp(cum), s)
        diff = jnp.where(causal[..., None], cum[..., :, None, :] - cum[..., None, :, :], -jnp.inf)
        scores = jnp.einsum('bhid,bhjd,bhijd->bhij', qc, kc, jnp.exp(diff))
        o = o_inter + jnp.einsum('bhij,bhje->bhie', scores, vc)
        s = jnp.exp(last[..., 0, :])[..., None] * s + jnp.einsum('bhcd,bhce->bhde', kc * jnp.exp(last - cum), vc)
        return s, o

    s0 = jnp.zeros((b, h, dk, dv), q.dtype)
    _, o = lax.scan(step, s0, (q, k, v, g))
    return o


def gdn_scan(q, k, v, g, beta):
    _, b, h, c, dk = q.shape
    dv = v.shape[-1]
    lower = jnp.tril(jnp.ones((c, c), bool))
    strict = jnp.tril(jnp.ones((c, c), bool), -1)
    gc = jnp.cumsum(g, axis=-1)
    decay = jnp.exp(jnp.where(lower, gc[..., :, None] - gc[..., None, :], -jnp.inf))
    k_beta = k * beta[..., None]
    v_beta = v * beta[..., None]
    a = jnp.where(strict, jnp.einsum('...id,...jd->...ij', k_beta, k) * decay, 0.0)
    eye = jnp.eye(c, dtype=q.dtype)
    t = lax.linalg.triangular_solve(eye + a, jnp.broadcast_to(eye, a.shape),
                                    left_side=True, lower=True, unit_diagonal=True)
    u = t @ v_beta
    w = t @ (k_beta * jnp.exp(gc)[..., None])
    qk = jnp.einsum('...id,...jd->...ij', q, k) * decay
    q_dec = q * jnp.exp(gc)[..., None]
    k_dec = k * jnp.exp(gc[..., -1:] - gc)[..., None]
    g_last = jnp.exp(gc[..., -1])

    def step(s, inp):
        qk_c, u_c, w_c, qd_c, kd_c, gl_c = inp
        v_new = u_c - w_c @ s
        o = qd_c @ s + qk_c @ v_new
        s = gl_c[..., None, None] * s + jnp.einsum('bhcd,bhce->bhde', kd_c, v_new)
        return s, o

    s0 = jnp.zeros((b, h, dk, dv), q.dtype)
    _, o = lax.scan(step, s0, (qk, u, w, q_dec, k_dec, g_last))
    return o


def head_norm_gate(o, gate, w, n_heads):
    b, l, f = o.shape
    on = rmsnorm(o.reshape(b, l, n_heads, f // n_heads), w)
    return (on * jax.nn.silu(gate.reshape(b, l, n_heads, f // n_heads))).reshape(b, l, f)


def token_mixer(h, w_in, gla_gate_w, gla_gate_b, gla_norm, gdn_conv, gdn_a_log, gdn_dt_bias, gdn_norm, w_out):
    b, l, _ = h.shape
    f32 = jnp.float32
    proj = (h @ w_in).astype(f32)
    sizes = (GLA_QK, GLA_QK, GLA_V, GLA_V, 2 * GLA_RANK,
             GDN_QK, GDN_QK, GDN_V, GDN_V, 2 * GDN_HEADS, 2 * GDN_HEADS)
    offs = np.cumsum(sizes)[:-1].tolist()
    gq, gk, gv, gg, glr, dq, dk, dv, dg, da, db = jnp.split(proj, offs, axis=-1)

    lr = glr.reshape(b, l, 2, GLA_RANK)
    logits = jnp.einsum('blsr,srk->blsk', lr, gla_gate_w.astype(f32)) + gla_gate_b.astype(f32)
    log_f = jax.nn.log_sigmoid(logits) / GLA_GATE_NORM
    gq = gq * (GLA_DK ** -0.5)
    o_f = gla_scan(heads_to_chunks(gq, GLA_HEADS), heads_to_chunks(gk, GLA_HEADS),
                   heads_to_chunks(gv, GLA_HEADS), heads_to_chunks(log_f[:, :, 0], GLA_HEADS))
    o_b = gla_scan(heads_to_chunks(flip(gq), GLA_HEADS), heads_to_chunks(flip(gk), GLA_HEADS),
                   heads_to_chunks(flip(gv), GLA_HEADS), heads_to_chunks(flip(log_f[:, :, 1]), GLA_HEADS))
    o_gla = chunks_to_heads(o_f) + flip(chunks_to_heads(o_b))
    o_gla = head_norm_gate(o_gla, gg, gla_norm, GLA_HEADS)

    qkv = jax.nn.silu(dwconv_centred(jnp.concatenate([dq, dk, dv], axis=-1), gdn_conv.astype(f32)))
    q, k, v = jnp.split(qkv, [GDN_QK, 2 * GDN_QK], axis=-1)
    q = (l2norm(q.reshape(b, l, GDN_HEADS, GDN_DK)) * (GDN_DK ** -0.5)).reshape(b, l, GDN_QK)
    k = l2norm(k.reshape(b, l, GDN_HEADS, GDN_DK)).reshape(b, l, GDN_QK)
    beta = jax.nn.sigmoid(db).reshape(b, l, 2, GDN_HEADS)
    gdec = -jnp.exp(gdn_a_log.astype(f32)) * jax.nn.softplus(
        da.reshape(b, l, 2, GDN_HEADS) + gdn_dt_bias.astype(f32))
    p_f = gdn_scan(heads_to_chunks(q, GDN_HEADS), heads_to_chunks(k, GDN_HEADS), heads_to_chunks(v, GDN_HEADS),
                   gates_to_chunks(gdec[:, :, 0]), gates_to_chunks(beta[:, :, 0]))
    p_b = gdn_scan(heads_to_chunks(flip(q), GDN_HEADS), heads_to_chunks(flip(k), GDN_HEADS),
                   heads_to_chunks(flip(v), GDN_HEADS),
                   gates_to_chunks(flip(gdec[:, :, 1])), gates_to_chunks(flip(beta[:, :, 1])))
    o_gdn = chunks_to_heads(p_f) + flip(chunks_to_heads(p_b))
    o_gdn = head_norm_gate(o_gdn, dg, gdn_norm, GDN_HEADS)

    o = jnp.concatenate([o_gla, o_gdn], axis=-1).astype(h.dtype)
    return o @ w_out


def conv_ffn(h, w_up, ffn_conv, w_down):
    u = dwconv_centred(h @ w_up, ffn_conv)
    val, gate = jnp.split(u, 2, axis=-1)
    return (jax.nn.silu(gate) * val) @ w_down


def encode(x, attn_norm, w_in, gla_gate_w, gla_gate_b, gla_norm, gdn_conv, gdn_a_log, gdn_dt_bias,
           gdn_norm, w_out, ffn_norm, w_up, ffn_conv, w_down, final_norm):
    for i in range(DEPTH):
        x = x + token_mixer(rmsnorm(x, attn_norm[i]), w_in[i], gla_gate_w[i], gla_gate_b[i], gla_norm[i],
                            gdn_conv[i], gdn_a_log[i], gdn_dt_bias[i], gdn_norm[i], w_out[i])
        x = x + conv_ffn(rmsnorm(x, ffn_norm[i]), w_up[i], ffn_conv[i], w_down[i])
    return rmsnorm(x, final_norm)


def setup_inputs(seed: int = 0) -> dict:
    key = jax.random.key(seed)
    ks = jax.random.split(key, 20)
    f32 = jnp.float32
    nrm = lambda k, shape, scale: jax.random.normal(k, shape, f32) * scale
    dt = jnp.exp(jax.random.uniform(ks[10], (DEPTH, 2, GDN_HEADS), f32, np.log(1e-3), np.log(1e-1)))
    return {
        'x_prompt': nrm(ks[0], (BATCH, SEQ, D_MODEL), 1.0),
        'x_sample': nrm(ks[1], (DEC_BATCH, DEC_SEQ, D_MODEL), 1.0),
        'attn_norm': 1.0 + nrm(ks[2], (DEPTH, D_MODEL), 0.02),
        'w_in': nrm(ks[3], (DEPTH, D_MODEL, D_IN), D_MODEL ** -0.5),
        'gla_gate_w': nrm(ks[4], (DEPTH, 2, GLA_RANK, GLA_QK), GLA_RANK ** -0.5),
        'gla_gate_b': nrm(ks[5], (DEPTH, 2, GLA_QK), 0.1),
        'gla_norm': 1.0 + nrm(ks[6], (DEPTH, GLA_DV), 0.02),
        'gdn_conv': nrm(ks[7], (DEPTH, SHORT_CONV, 2 * GDN_QK + GDN_V), SHORT_CONV ** -0.5),
        'gdn_a_log': jnp.log(jax.random.uniform(ks[8], (DEPTH, 2, GDN_HEADS), f32, 1.0, 16.0)),
        'gdn_dt_bias': dt + jnp.log(-jnp.expm1(-dt)),
        'gdn_norm': 1.0 + nrm(ks[9], (DEPTH, GDN_DV), 0.02),
        'w_out': nrm(ks[11], (DEPTH, MIX_WIDTH, D_MODEL), MIX_WIDTH ** -0.5),
        'ffn_norm': 1.0 + nrm(ks[12], (DEPTH, D_MODEL), 0.02),
        'w_up': nrm(ks[13], (DEPTH, D_MODEL, 2 * D_FF), D_MODEL ** -0.5),
        'ffn_conv': nrm(ks[14], (DEPTH, FFN_CONV, 2 * D_FF), FFN_CONV ** -0.5),
        'w_down': nrm(ks[15], (DEPTH, D_FF, D_MODEL), D_FF ** -0.5),
        'final_norm': 1.0 + nrm(ks[16], (D_MODEL,), 0.02),
    }


def reference(x_prompt, x_sample, attn_norm, w_in, gla_gate_w, gla_gate_b, gla_norm, gdn_conv, gdn_a_log,
              gdn_dt_bias, gdn_norm, w_out, ffn_norm, w_up, ffn_conv, w_down, final_norm):
    y_prompt = encode(x_prompt, attn_norm, w_in, gla_gate_w, gla_gate_b, gla_norm, gdn_conv, gdn_a_log,
                      gdn_dt_bias, gdn_norm, w_out, ffn_norm, w_up, ffn_conv, w_down, final_norm)
    y_sample = encode(x_sample, attn_norm, w_in, gla_gate_w, gla_gate_b, gla_norm, gdn_conv, gdn_a_log,
                      gdn_dt_bias, gdn_norm, w_out, ffn_norm, w_up, ffn_conv, w_down, final_norm)
    return (y_prompt, y_sample)
```

```python
import functools

import jax
import jax.numpy as jnp
from jax import lax
from jax.experimental import pallas as pl
from jax.experimental.pallas import tpu as pltpu

D_MODEL = 1024
CHUNK = 64
GLA_HEADS = 4
GLA_DK = 64
GLA_DV = 128
GLA_RANK = 16
GLA_GATE_NORM = 16.0
GDN_HEADS = 4
GDN_DK = 128
GDN_DV = 128
D_FF = 2816
EPS = 1e-6

GLA_QK = GLA_HEADS * GLA_DK
GLA_V = GLA_HEADS * GLA_DV
GDN_QK = GDN_HEADS * GDN_DK
GDN_V = GDN_HEADS * GDN_DV
GDN_CONV_W = 2 * GDN_QK + GDN_V

SM_W = 128
SM_GDEC = 2 * GLA_RANK
SM_BETA = SM_GDEC + 2 * GDN_HEADS

HALO = 8
FF_TILE = 256
SUB = 16

VMEM_LIMIT = 56 * 1024 * 1024

F32 = jnp.float32
BF16 = jnp.bfloat16


def _dot(a, b):
    return jnp.dot(a.astype(BF16), b.astype(BF16), preferred_element_type=F32)


def _dot_nt(a, b):
    return lax.dot_general(a.astype(BF16), b.astype(BF16), (((1,), (1,)), ((), ())),
                           preferred_element_type=F32)


def _dot_tn(a, b):
    return lax.dot_general(a.astype(BF16), b.astype(BF16), (((0,), (0,)), ((), ())),
                           preferred_element_type=F32)


def _split(a):
    hi = a.astype(BF16)
    lo = (a - hi.astype(F32)).astype(BF16)
    return hi, lo


def _dot3(a, b):
    ah, al = _split(a)
    bh, bl = _split(b)
    return (jnp.dot(ah, bh, preferred_element_type=F32)
            + jnp.dot(ah, bl, preferred_element_type=F32)
            + jnp.dot(al, bh, preferred_element_type=F32))


def _sigmoid(x):
    return 1.0 / (1.0 + jnp.exp(-x))


def _softplus(x):
    return jnp.maximum(x, 0.0) + jnp.log1p(jnp.exp(-jnp.abs(x)))


def _rms(x, w):
    return x * lax.rsqrt(jnp.mean(x * x, axis=-1, keepdims=True) + EPS) * w


def _inproj_kernel(x_ref, xp_ref, xn_ref, nw_ref, wa_ref, wg_ref, wc_ref, ws_ref, gw_ref, gb_ref,
                   conv_ref, alog_ref, dtb_ref,
                   qkv_ref, lff_ref, lfb_ref, gdn_ref, gates_ref, small_ref, smallt_ref):
    j = pl.program_id(1)
    nj = pl.num_programs(1)
    tm = x_ref.shape[1]

    xe = jnp.concatenate([x_ref[0], xp_ref[0], xn_ref[0]], axis=0)
    he = _rms(xe, nw_ref[...]).astype(BF16)
    h = he[:tm]

    a = jnp.dot(h, wa_ref[...], preferred_element_type=F32)
    qkv_ref[0, :, :GLA_QK] = (a[:, :GLA_QK] * (GLA_DK ** -0.5)).astype(BF16)
    qkv_ref[0, :, GLA_QK:] = a[:, GLA_QK:].astype(BF16)

    gates_ref[0] = jnp.dot(h, wg_ref[...], preferred_element_type=F32).astype(BF16)

    s = jnp.dot(h, ws_ref[...], preferred_element_type=F32)
    logits = jnp.dot(s.astype(BF16), gw_ref[...], preferred_element_type=F32) + gb_ref[...]
    log_f = (jnp.minimum(logits, 0.0) - jnp.log1p(jnp.exp(-jnp.abs(logits)))) * (1.0 / GLA_GATE_NORM)
    lff_ref[0] = log_f[:, :GLA_QK]
    lfb_ref[0] = log_f[:, GLA_QK:]

    lane = lax.broadcasted_iota(jnp.int32, s.shape, 1)
    gdec = -jnp.exp(alog_ref[...]) * _softplus(s + dtb_ref[...])
    beta = _sigmoid(s)
    small = jnp.where((lane >= SM_GDEC) & (lane < SM_BETA), gdec,
                      jnp.where((lane >= SM_BETA) & (lane < SM_BETA + 2 * GDN_HEADS), beta, 0.0))
    small_ref[0] = small
    smallt_ref[0] = jnp.transpose(small)[SM_GDEC:SM_GDEC + 4 * GDN_HEADS]

    ce = jnp.dot(he, wc_ref[...], preferred_element_type=F32)
    c = ce[:tm]
    c_before = jnp.where(j == 0, 0.0, ce[tm + HALO - 1:tm + HALO])
    c_after = jnp.where(j == nj - 1, 0.0, ce[tm + HALO:tm + HALO + 1])
    row = lax.broadcasted_iota(jnp.int32, c.shape, 0)
    c_prev = jnp.where(row == 0, c_before, pltpu.roll(c, 1, 0))
    c_next = jnp.where(row == tm - 1, c_after, pltpu.roll(c, tm - 1, 0))
    cw = conv_ref[...]
    y = c_prev * cw[0:1] + c * cw[1:2] + c_next * cw[2:3]
    y = y * _sigmoid(y)
    for i in range(2 * GDN_HEADS):
        seg = y[:, i * GDN_DK:(i + 1) * GDN_DK]
        seg = seg * lax.rsqrt(jnp.sum(seg * seg, axis=-1, keepdims=True) + EPS)
        if i < GDN_HEADS:
            seg = seg * (GDN_DK ** -0.5)
        gdn_ref[0, :, i * GDN_DK:(i + 1) * GDN_DK] = seg.astype(BF16)
    gdn_ref[0, :, 2 * GDN_QK:] = y[:, 2 * GDN_QK:].astype(BF16)


def _halo_specs(tm, seq, width):
    per = tm // HALO
    last = seq // HALO - 1
    prev = pl.BlockSpec((1, HALO, width), lambda b, j: (b, jnp.maximum(j * per - 1, 0), 0))
    nxt = pl.BlockSpec((1, HALO, width), lambda b, j: (b, jnp.minimum((j + 1) * per, last), 0))
    return prev, nxt


def _const_spec(shape):
    return pl.BlockSpec(shape, lambda b, j: (0,) * len(shape))


def _inproj(x, nw, wa, wg, wc, ws, gw, gb, conv, alog, dtb, tm):
    bsz, seq, _ = x.shape
    prev, nxt = _halo_specs(tm, seq, D_MODEL)
    tok = lambda w: pl.BlockSpec((1, tm, w), lambda b, j: (b, j, 0))
    out_shape = (
        jax.ShapeDtypeStruct((bsz, seq, GLA_QK * 2 + GLA_V), BF16),
        jax.ShapeDtypeStruct((bsz, seq, GLA_QK), F32),
        jax.ShapeDtypeStruct((bsz, seq, GLA_QK), F32),
        jax.ShapeDtypeStruct((bsz, seq, GDN_CONV_W), BF16),
        jax.ShapeDtypeStruct((bsz, seq, GLA_V + GDN_V), BF16),
        jax.ShapeDtypeStruct((bsz, seq, SM_W), F32),
        jax.ShapeDtypeStruct((bsz, 4 * GDN_HEADS, seq), F32),
    )
    out_specs = (tok(GLA_QK * 2 + GLA_V), tok(GLA_QK), tok(GLA_QK), tok(GDN_CONV_W),
                 tok(GLA_V + GDN_V), tok(SM_W),
                 pl.BlockSpec((1, 4 * GDN_HEADS, tm), lambda b, j: (b, 0, j)))
    in_specs = [tok(D_MODEL), prev, nxt] + [_const_spec(a.shape) for a in
                                            (nw, wa, wg, wc, ws, gw, gb, conv, alog, dtb)]
    return pl.pallas_call(
        _inproj_kernel,
        grid=(bsz, seq // tm),
        in_specs=in_specs,
        out_specs=out_specs,
        out_shape=out_shape,
        compiler_params=pltpu.CompilerParams(
            dimension_semantics=("arbitrary", "arbitrary"), vmem_limit_bytes=VMEM_LIMIT),
        name="inproj",
    )(x, x, x, nw, wa, wg, wc, ws, gw, gb, conv, alog, dtb)


def _mm(a, b):
    return jnp.dot(a, b, preferred_element_type=F32)


def _dot3_split(a, b):
    return _mm(a[0], b[0]) + _mm(a[0], b[1]) + _mm(a[1], b[0])


def _inv_unit_triangular(a_list, bd_mask, eye):
    a_bd = [jnp.where(bd_mask, a, 0.0) for a in a_list]
    a_off = [a - b for a, b in zip(a_list, a_bd)]
    x1 = [_split(-b) for b in a_bd]
    d = [_split(eye - b) for b in a_bd]
    x2 = [_dot3_split(v, v) for v in x1]
    x2s = [_split(v) for v in x2]
    x4 = [_dot3_split(v, v) for v in x2s]
    d = [_split(di[0].astype(F32) + di[1].astype(F32) + _dot3_split(di, v)) for di, v in zip(d, x2s)]
    x4s = [_split(v) for v in x4]
    x8 = [_dot3_split(v, v) for v in x4s]
    d = [_split(di[0].astype(F32) + di[1].astype(F32) + _dot3_split(di, v)) for di, v in zip(d, x4s)]
    x8s = [_split(v) for v in x8]
    d = [di[0].astype(F32) + di[1].astype(F32) + _dot3_split(di, v) for di, v in zip(d, x8s)]
    db = [v.astype(BF16) for v in d]
    n = [_mm(v, ao.astype(BF16)) for v, ao in zip(db, a_off)]
    nb = [v.astype(BF16) for v in n]
    n2 = [_mm(v, v) for v in nb]
    q = [eye - v for v in n]
    q = [qi + _mm(qi.astype(BF16), v.astype(BF16)) for qi, v in zip(q, n2)]
    return [_mm(qi.astype(BF16), v) for qi, v in zip(q, db)]


def _scan_kernel(qkv_f, lf_f, gdn_f, sm_f, smt_f, qkv_b, lf_b, gdn_b, sm_b, smt_b,
                 o_f, o_b, s_gla, s_gdn):
    @pl.when(pl.program_id(1) == 0)
    def _():
        s_gla[...] = jnp.zeros_like(s_gla)
        s_gdn[...] = jnp.zeros_like(s_gdn)

    n_chunks = qkv_f.shape[1] // CHUNK
    ri = lax.broadcasted_iota(jnp.int32, (CHUNK, CHUNK), 0)
    ci = lax.broadcasted_iota(jnp.int32, (CHUNK, CHUNK), 1)
    lower = ri >= ci
    upper = ri <= ci
    eye = jnp.where(ri == ci, 1.0, 0.0).astype(F32)
    bd_mask = (ri // SUB) == (ci // SUB)
    tri_lo = jnp.where(lower, 1.0, 0.0).astype(BF16)
    tri_up = jnp.where(upper, 1.0, 0.0).astype(BF16)
    ones_bc = jnp.ones((CHUNK, GLA_DV), BF16)
    tn = (((0,), (0,)), ((), ()))

    refs = ((qkv_f, lf_f, gdn_f, sm_f, smt_f, o_f), (qkv_b, lf_b, gdn_b, sm_b, smt_b, o_b))
    consts = ((tri_lo, tri_up, lower, ri > ci, CHUNK - 1), (tri_up, tri_lo, upper, ri < ci, 0))

    units = []
    for step in range(n_chunks):
        for d in range(2):
            qkv, lf, gdn, sm, smt, _ = refs[d]
            tri, tri_t, m_incl, m_strict, last = consts[d]
            c = step if d == 0 else n_chunks - 1 - step
            rows = slice(c * CHUNK, (c + 1) * CHUNK)
            x = qkv[0, rows, :].astype(F32)
            lf_hi, lf_lo = _split(lf[0, rows, :])
            small = sm[0, rows, :]
            sm_hi, sm_lo = _split(small)
            smt_hi, smt_lo = _split(smt[0, :, rows])
            units.append(dict(
                d=d, rows=rows, q=x[:, :GLA_QK], k=x[:, GLA_QK:2 * GLA_QK], v=x[:, 2 * GLA_QK:],
                y=gdn[0, rows, :].astype(F32), small=small,
                cum=_mm(tri, lf_hi) + _mm(tri, lf_lo),
                total=(lax.dot_general(lf_hi, ones_bc, tn, preferred_element_type=F32)
                       + lax.dot_general(lf_lo, ones_bc, tn, preferred_element_type=F32)),
                g_cols=_mm(tri, sm_hi) + _mm(tri, sm_lo),
                g_rows=_mm(smt_hi, tri_t) + _mm(smt_lo, tri_t)))

    for u in units:
        _, _, m_incl, _, last = consts[u["d"]]
        cum = u["cum"]
        q_in = u["q"] * jnp.exp(cum)
        k_in = u["k"] * jnp.exp(-cum)
        k_out = u["k"] * jnp.exp(cum[last:last + 1] - cum)
        u["q_in"] = q_in
        att, kv = [], []
        for h in range(GLA_HEADS):
            ks = slice(h * GLA_DK, (h + 1) * GLA_DK)
            vs = slice(h * GLA_DV, (h + 1) * GLA_DV)
            att.append(jnp.where(m_incl, _dot_nt(q_in[:, ks], k_in[:, ks]), 0.0))
            kv.append(_dot_tn(k_out[:, ks], u["v"][:, vs]))
        u["att"] = att
        u["kv"] = jnp.concatenate(kv, axis=0)

    chains = []
    for u in units:
        d = u["d"]
        _, _, m_incl, m_strict, last = consts[d]
        y = u["y"]
        for h in range(GDN_HEADS):
            q = y[:, h * GDN_DK:(h + 1) * GDN_DK]
            k = y[:, GDN_QK + h * GDN_DK:GDN_QK + (h + 1) * GDN_DK]
            v = y[:, 2 * GDN_QK + h * GDN_DV:2 * GDN_QK + (h + 1) * GDN_DV]
            lane = SM_GDEC + GDN_HEADS * d + h
            lane_b = SM_BETA + GDN_HEADS * d + h
            r = GDN_HEADS * d + h
            g_col = u["g_cols"][:, lane:lane + 1]
            g_row = u["g_rows"][r:r + 1, :]
            b_col = u["small"][:, lane_b:lane_b + 1]
            decay = jnp.where(m_incl, jnp.exp(g_col - g_row), 0.0)
            e_g = jnp.exp(g_col)
            g_last = g_row[:, last:last + 1]
            p = _dot_nt(jnp.concatenate([q, k], axis=0), k)
            chains.append(dict(
                u=u, h=h, qk=p[:CHUNK] * decay,
                a=jnp.where(m_strict, p[CHUNK:] * b_col * decay, 0.0),
                rhs=jnp.concatenate([v * b_col, k * (b_col * e_g)], axis=1),
                q_dec=q * e_g, k_dec=k * jnp.exp(g_last - g_col), e_last=jnp.exp(g_last)))
    t_list = _inv_unit_triangular([c["a"] for c in chains], bd_mask, eye)
    for c, t in zip(chains, t_list):
        c["uw"] = _dot(t, c["rhs"])

    sg = [s_gla[0], s_gla[1]]
    sd = [[s_gdn[d, h] for h in range(GDN_HEADS)] for d in range(2)]
    for step in range(n_chunks):
        for d in range(2):
            u = units[2 * step + d]
            o_ref = refs[d][5]
            s = sg[d]
            for h in range(GLA_HEADS):
                ks = slice(h * GLA_DK, (h + 1) * GLA_DK)
                vs = slice(h * GLA_DV, (h + 1) * GLA_DV)
                lhs = jnp.concatenate([u["q_in"][:, ks], u["att"][h]], axis=1)
                rhs = jnp.concatenate([s[ks], u["v"][:, vs]], axis=0)
                o_ref[0, u["rows"], vs] = _dot(lhs, rhs)
            sg[d] = jnp.exp(u["total"]) * s + u["kv"]
        cs = chains[2 * GDN_HEADS * step:2 * GDN_HEADS * (step + 1)]
        r = [_dot(jnp.concatenate([c["uw"][:, GDN_DV:], c["q_dec"]], axis=0), sd[c["u"]["d"]][c["h"]])
             for c in cs]
        v_new = [c["uw"][:, :GDN_DV] - ri_[:CHUNK] for c, ri_ in zip(cs, r)]
        for c, ri_, vn in zip(cs, r, v_new):
            d, h = c["u"]["d"], c["h"]
            o_ref = refs[d][5]
            o_ref[0, c["u"]["rows"], GLA_V + h * GDN_DV:GLA_V + (h + 1) * GDN_DV] = (
                ri_[CHUNK:] + _dot(c["qk"], vn))
            sd[d][h] = c["e_last"] * sd[d][h] + _dot_tn(c["k_dec"], vn)
    for d in range(2):
        s_gla[d] = sg[d]
        for h in range(GDN_HEADS):
            s_gdn[d, h] = sd[d][h]


def _scan(qkv, lff, lfb, gdn, small, small_t, tb):
    bsz, seq, _ = qkv.shape
    nb = seq // tb
    fwd = lambda w: pl.BlockSpec((1, tb, w), lambda b, j: (b, j, 0))
    bwd = lambda w: pl.BlockSpec((1, tb, w), lambda b, j: (b, nb - 1 - j, 0))
    rows = 4 * GDN_HEADS
    in_specs = [fwd(qkv.shape[2]), fwd(GLA_QK), fwd(GDN_CONV_W), fwd(SM_W),
                pl.BlockSpec((1, rows, tb), lambda b, j: (b, 0, j)),
                bwd(qkv.shape[2]), bwd(GLA_QK), bwd(GDN_CONV_W), bwd(SM_W),
                pl.BlockSpec((1, rows, tb), lambda b, j: (b, 0, nb - 1 - j))]
    width = GLA_V + GDN_V
    return pl.pallas_call(
        _scan_kernel,
        grid=(bsz, nb),
        in_specs=in_specs,
        out_specs=(fwd(width), bwd(width)),
        out_shape=(jax.ShapeDtypeStruct((bsz, seq, width), F32),) * 2,
        scratch_shapes=[pltpu.VMEM((2, GLA_QK, GLA_DV), F32),
                        pltpu.VMEM((2, GDN_HEADS, GDN_DK, GDN_DV), F32)],
        compiler_params=pltpu.CompilerParams(
            dimension_semantics=("arbitrary", "arbitrary"), vmem_limit_bytes=VMEM_LIMIT),
        name="scan",
    )(qkv, lff, gdn, small, small_t, qkv, lfb, gdn, small, small_t)


def _outproj_kernel(of_ref, ob_ref, gates_ref, x_ref, gn_ref, wout_ref, x1_ref):
    o = of_ref[0] + ob_ref[0]
    g = gates_ref[0].astype(F32)
    gn = gn_ref[...]
    parts = []
    for h in range(GLA_HEADS + GDN_HEADS):
        hs = slice(h * GLA_DV, (h + 1) * GLA_DV)
        gate = g[:, hs]
        parts.append(_rms(o[:, hs], gn[:, hs]) * (gate * _sigmoid(gate)))
    y = jnp.concatenate(parts, axis=-1).astype(BF16)
    x1_ref[0] = x_ref[0] + jnp.dot(y, wout_ref[...], preferred_element_type=F32)


def _outproj(o_f, o_b, gates, x, gn, wout, tm):
    bsz, seq, _ = x.shape
    tok = pl.BlockSpec((1, tm, D_MODEL), lambda b, j: (b, j, 0))
    return pl.pallas_call(
        _outproj_kernel,
        grid=(bsz, seq // tm),
        in_specs=[tok, tok, tok, tok, _const_spec(gn.shape), _const_spec(wout.shape)],
        out_specs=tok,
        out_shape=jax.ShapeDtypeStruct((bsz, seq, D_MODEL), F32),
        compiler_params=pltpu.CompilerParams(
            dimension_semantics=("arbitrary", "arbitrary"), vmem_limit_bytes=VMEM_LIMIT),
        name="outproj",
    )(o_f, o_b, gates, x, gn, wout)


def _ffn_kernel(x_ref, xp_ref, xn_ref, nw_ref, wup_ref, conv_ref, wdown_ref, fw_ref, out_ref, acc_ref):
    j = pl.program_id(1)
    nj = pl.num_programs(1)
    tm = x_ref.shape[1]
    x = x_ref[0]
    xe = jnp.concatenate([x, xp_ref[0], xn_ref[0]], axis=0)
    he = _rms(xe, nw_ref[...]).astype(BF16)
    row = lax.broadcasted_iota(jnp.int32, (tm, 2 * FF_TILE), 0)
    first = row == 0
    final = row == tm - 1
    acc_ref[...] = jnp.zeros_like(acc_ref)

    def body(t, carry):
        ue = jnp.dot(he, wup_ref[t], preferred_element_type=F32)
        u = ue[:tm]
        u_before = jnp.where(j == 0, 0.0, ue[tm + HALO - 1:tm + HALO])
        u_after = jnp.where(j == nj - 1, 0.0, ue[tm + HALO:tm + HALO + 1])
        u_prev = jnp.where(first, u_before, pltpu.roll(u, 1, 0))
        u_next = jnp.where(final, u_after, pltpu.roll(u, tm - 1, 0))
        cw = conv_ref[t]
        cu = u_prev * cw[0:1] + u * cw[1:2] + u_next * cw[2:3]
        gate = cu[:, FF_TILE:]
        act = (gate * _sigmoid(gate) * cu[:, :FF_TILE]).astype(BF16)
        acc_ref[...] += jnp.dot(act, wdown_ref[t], preferred_element_type=F32)
        return carry

    lax.fori_loop(0, D_FF // FF_TILE, body, 0)
    out_ref[0] = _rms(x + acc_ref[...], fw_ref[...])


def _ffn(x1, nw, wup, conv, wdown, fw, tm):
    bsz, seq, _ = x1.shape
    prev, nxt = _halo_specs(tm, seq, D_MODEL)
    tok = pl.BlockSpec((1, tm, D_MODEL), lambda b, j: (b, j, 0))
    return pl.pallas_call(
        _ffn_kernel,
        grid=(bsz, seq // tm),
        in_specs=[tok, prev, nxt] + [_const_spec(a.shape) for a in (nw, wup, conv, wdown, fw)],
        out_specs=tok,
        out_shape=jax.ShapeDtypeStruct((bsz, seq, D_MODEL), F32),
        scratch_shapes=[pltpu.VMEM((tm, D_MODEL), F32)],
        compiler_params=pltpu.CompilerParams(
            dimension_semantics=("arbitrary", "arbitrary"), vmem_limit_bytes=VMEM_LIMIT),
        name="ffn",
    )(x1, x1, x1, nw, wup, conv, wdown, fw)


def _stage_params(attn_norm, w_in, gla_gate_w, gla_gate_b, gla_norm, gdn_conv, gdn_a_log, gdn_dt_bias,
                  gdn_norm, w_out, ffn_norm, w_up, ffn_conv, w_down, final_norm):
    w = w_in[0]
    o_gg = 2 * GLA_QK + GLA_V
    o_lr = o_gg + GLA_V
    o_dq = o_lr + 2 * GLA_RANK
    o_dg = o_dq + GDN_CONV_W
    o_da = o_dg + GDN_V
    wa = w[:, :o_gg].astype(BF16)
    wg = jnp.concatenate([w[:, o_gg:o_lr], w[:, o_dg:o_da]], axis=1).astype(BF16)
    wc = w[:, o_dq:o_dg].astype(BF16)
    ws = jnp.concatenate([w[:, o_lr:o_dq], w[:, o_da:],
                          jnp.zeros((D_MODEL, SM_W - 2 * GLA_RANK - 4 * GDN_HEADS), F32)], axis=1).astype(BF16)
    gw = jnp.zeros((SM_W, 2 * GLA_QK), F32)
    gw = gw.at[:GLA_RANK, :GLA_QK].set(gla_gate_w[0, 0])
    gw = gw.at[GLA_RANK:2 * GLA_RANK, GLA_QK:].set(gla_gate_w[0, 1]).astype(BF16)
    gb = gla_gate_b[0].reshape(1, 2 * GLA_QK)
    pad8 = lambda a: jnp.concatenate([a, jnp.zeros((HALO - a.shape[0],) + a.shape[1:], a.dtype)], axis=0)
    conv = pad8(gdn_conv[0])
    lanes = jnp.zeros((1, SM_W), F32)
    alog = lanes.at[0, SM_GDEC:SM_BETA].set(gdn_a_log[0].reshape(-1))
    dtb = lanes.at[0, SM_GDEC:SM_BETA].set(gdn_dt_bias[0].reshape(-1))
    gn = jnp.concatenate([jnp.tile(gla_norm[0], GLA_HEADS), jnp.tile(gdn_norm[0], GDN_HEADS)]).reshape(1, -1)
    nt = D_FF // FF_TILE
    up = w_up[0].reshape(D_MODEL, 2, nt, FF_TILE).transpose(2, 0, 1, 3).reshape(nt, D_MODEL, 2 * FF_TILE)
    fconv = ffn_conv[0].reshape(3, 2, nt, FF_TILE).transpose(2, 0, 1, 3).reshape(nt, 3, 2 * FF_TILE)
    fconv = jnp.concatenate([fconv, jnp.zeros((nt, HALO - 3, 2 * FF_TILE), F32)], axis=1)
    down = w_down[0].reshape(nt, FF_TILE, D_MODEL)
    return dict(
        attn_norm=attn_norm[0].reshape(1, -1), wa=wa, wg=wg, wc=wc, ws=ws, gw=gw, gb=gb, conv=conv,
        alog=alog, dtb=dtb, gn=gn, wout=w_out[0].astype(BF16), ffn_norm=ffn_norm[0].reshape(1, -1),
        wup=up.astype(BF16), fconv=fconv, wdown=down.astype(BF16), final_norm=final_norm.reshape(1, -1))


def _tile(seq, want):
    t = min(want, seq)
    assert seq % t == 0 and t % 16 == 0
    return t


def _encode(x, p):
    seq = x.shape[1]
    tm = _tile(seq, 512)
    tb = _tile(seq, 128)
    qkv, lff, lfb, gdn, gates, small, small_t = _inproj(
        x, p["attn_norm"], p["wa"], p["wg"], p["wc"], p["ws"], p["gw"], p["gb"], p["conv"],
        p["alog"], p["dtb"], tm)
    o_f, o_b = _scan(qkv, lff, lfb, gdn, small, small_t, tb)
    x1 = _outproj(o_f, o_b, gates, x, p["gn"], p["wout"], tm)
    return _ffn(x1, p["ffn_norm"], p["wup"], p["fconv"], p["wdown"], p["final_norm"], tm)


def kernel(x_prompt, x_sample, attn_norm, w_in, gla_gate_w, gla_gate_b, gla_norm, gdn_conv, gdn_a_log,
           gdn_dt_bias, gdn_norm, w_out, ffn_norm, w_up, ffn_conv, w_down, final_norm):
    p = _stage_params(attn_norm, w_in, gla_gate_w, gla_gate_b, gla_norm, gdn_conv, gdn_a_log, gdn_dt_bias,
                      gdn_norm, w_out, ffn_norm, w_up, ffn_conv, w_down, final_norm)
    return (_encode(x_prompt, p), _encode(x_sample, p))
```

```python
import functools

import jax
import jax.numpy as jnp
from jax import lax
from jax.experimental import pallas as pl
from jax.experimental.pallas import tpu as pltpu

D_MODEL = 1024
CHUNK = 64
GLA_HEADS = 4
GLA_DK = 64
GLA_DV = 128
GLA_RANK = 16
GLA_GATE_NORM = 16.0
GDN_HEADS = 4
GDN_DK = 128
GDN_DV = 128
D_FF = 2816
EPS = 1e-6

GLA_QK = GLA_HEADS * GLA_DK
GLA_V = GLA_HEADS * GLA_DV
GDN_QK = GDN_HEADS * GDN_DK
GDN_V = GDN_HEADS * GDN_DV
GDN_CONV_W = 2 * GDN_QK + GDN_V

SM_W = 128
SM_GDEC = 2 * GLA_RANK
SM_BETA = SM_GDEC + 2 * GDN_HEADS

HALO = 8
FF_TILE = 256
SUB = 16

VMEM_LIMIT = 56 * 1024 * 1024

F32 = jnp.float32
BF16 = jnp.bfloat16


def _dot(a, b):
    return jnp.dot(a.astype(BF16), b.astype(BF16), preferred_element_type=F32)


def _dot_nt(a, b):
    return lax.dot_general(a.astype(BF16), b.astype(BF16), (((1,), (1,)), ((), ())),
                           preferred_element_type=F32)


def _dot_tn(a, b):
    return lax.dot_general(a.astype(BF16), b.astype(BF16), (((0,), (0,)), ((), ())),
                           preferred_element_type=F32)


def _split(a):
    hi = a.astype(BF16)
    lo = (a - hi.astype(F32)).astype(BF16)
    return hi, lo


def _sigmoid(x):
    return 1.0 / (1.0 + jnp.exp(-x))


def _softplus(x):
    return jnp.maximum(x, 0.0) + jnp.log1p(jnp.exp(-jnp.abs(x)))


def _rms(x, w):
    return x * lax.rsqrt(jnp.mean(x * x, axis=-1, keepdims=True) + EPS) * w


def _inproj_kernel(x_ref, xp_ref, xn_ref, nw_ref, wa_ref, wg_ref, wc_ref, ws_ref, gw_ref, gb_ref,
                   conv_ref, alog_ref, dtb_ref,
                   qkv_ref, lff_ref, lfb_ref, gdn_ref, gates_ref, small_ref, smallt_ref):
    j = pl.program_id(1)
    nj = pl.num_programs(1)
    tm = x_ref.shape[1]

    xe = jnp.concatenate([x_ref[0], xp_ref[0], xn_ref[0]], axis=0)
    he = _rms(xe, nw_ref[...]).astype(BF16)
    h = he[:tm]

    a = jnp.dot(h, wa_ref[...], preferred_element_type=F32)
    qkv_ref[0, :, :GLA_QK] = (a[:, :GLA_QK] * (GLA_DK ** -0.5)).astype(BF16)
    qkv_ref[0, :, GLA_QK:] = a[:, GLA_QK:].astype(BF16)

    gates_ref[0] = jnp.dot(h, wg_ref[...], preferred_element_type=F32).astype(BF16)

    s = jnp.dot(h, ws_ref[...], preferred_element_type=F32)
    logits = jnp.dot(s.astype(BF16), gw_ref[...], preferred_element_type=F32) + gb_ref[...]
    log_f = (jnp.minimum(logits, 0.0) - jnp.log1p(jnp.exp(-jnp.abs(logits)))) * (1.0 / GLA_GATE_NORM)
    lff_ref[0] = log_f[:, :GLA_QK]
    lfb_ref[0] = log_f[:, GLA_QK:]

    lane = lax.broadcasted_iota(jnp.int32, s.shape, 1)
    gdec = -jnp.exp(alog_ref[...]) * _softplus(s + dtb_ref[...])
    beta = _sigmoid(s)
    small = jnp.where((lane >= SM_GDEC) & (lane < SM_BETA), gdec,
                      jnp.where((lane >= SM_BETA) & (lane < SM_BETA + 2 * GDN_HEADS), beta, 0.0))
    small_ref[0] = small
    smallt_ref[0] = jnp.transpose(small)[SM_GDEC:SM_GDEC + 4 * GDN_HEADS]

    ce = jnp.dot(he, wc_ref[...], preferred_element_type=F32)
    c = ce[:tm]
    c_before = jnp.where(j == 0, 0.0, ce[tm + HALO - 1:tm + HALO])
    c_after = jnp.where(j == nj - 1, 0.0, ce[tm + HALO:tm + HALO + 1])
    row = lax.broadcasted_iota(jnp.int32, c.shape, 0)
    c_prev = jnp.where(row == 0, c_before, pltpu.roll(c, 1, 0))
    c_next = jnp.where(row == tm - 1, c_after, pltpu.roll(c, tm - 1, 0))
    cw = conv_ref[...]
    y = c_prev * cw[0:1] + c * cw[1:2] + c_next * cw[2:3]
    y = y * _sigmoid(y)
    for i in range(2 * GDN_HEADS):
        seg = y[:, i * GDN_DK:(i + 1) * GDN_DK]
        seg = seg * lax.rsqrt(jnp.sum(seg * seg, axis=-1, keepdims=True) + EPS)
        if i < GDN_HEADS:
            seg = seg * (GDN_DK ** -0.5)
        gdn_ref[0, :, i * GDN_DK:(i + 1) * GDN_DK] = seg.astype(BF16)
    gdn_ref[0, :, 2 * GDN_QK:] = y[:, 2 * GDN_QK:].astype(BF16)


def _halo_specs(tm, seq, width):
    per = tm // HALO
    last = seq // HALO - 1
    prev = pl.BlockSpec((1, HALO, width), lambda b, j: (b, jnp.maximum(j * per - 1, 0), 0))
    nxt = pl.BlockSpec((1, HALO, width), lambda b, j: (b, jnp.minimum((j + 1) * per, last), 0))
    return prev, nxt


def _const_spec(shape):
    return pl.BlockSpec(shape, lambda b, j: (0,) * len(shape))


def _inproj(x, nw, wa, wg, wc, ws, gw, gb, conv, alog, dtb, tm):
    bsz, seq, _ = x.shape
    prev, nxt = _halo_specs(tm, seq, D_MODEL)
    tok = lambda w: pl.BlockSpec((1, tm, w), lambda b, j: (b, j, 0))
    out_shape = (
        jax.ShapeDtypeStruct((bsz, seq, GLA_QK * 2 + GLA_V), BF16),
        jax.ShapeDtypeStruct((bsz, seq, GLA_QK), F32),
        jax.ShapeDtypeStruct((bsz, seq, GLA_QK), F32),
        jax.ShapeDtypeStruct((bsz, seq, GDN_CONV_W), BF16),
        jax.ShapeDtypeStruct((bsz, seq, GLA_V + GDN_V), BF16),
        jax.ShapeDtypeStruct((bsz, seq, SM_W), F32),
        jax.ShapeDtypeStruct((bsz, 4 * GDN_HEADS, seq), F32),
    )
    out_specs = (tok(GLA_QK * 2 + GLA_V), tok(GLA_QK), tok(GLA_QK), tok(GDN_CONV_W),
                 tok(GLA_V + GDN_V), tok(SM_W),
                 pl.BlockSpec((1, 4 * GDN_HEADS, tm), lambda b, j: (b, 0, j)))
    in_specs = [tok(D_MODEL), prev, nxt] + [_const_spec(a.shape) for a in
                                            (nw, wa, wg, wc, ws, gw, gb, conv, alog, dtb)]
    return pl.pallas_call(
        _inproj_kernel,
        grid=(bsz, seq // tm),
        in_specs=in_specs,
        out_specs=out_specs,
        out_shape=out_shape,
        compiler_params=pltpu.CompilerParams(
            dimension_semantics=("arbitrary", "arbitrary"), vmem_limit_bytes=VMEM_LIMIT),
        name="inproj",
    )(x, x, x, nw, wa, wg, wc, ws, gw, gb, conv, alog, dtb)


def _mm(a, b):
    return jnp.dot(a, b, preferred_element_type=F32)


def _inv_unit_triangular(a_list, bd_mask, eye):
    a_bd = [jnp.where(bd_mask, a, 0.0) for a in a_list]
    a_off = [(a - b).astype(BF16) for a, b in zip(a_list, a_bd)]
    d = [eye - b for b in a_bd]
    x1 = [(-b).astype(BF16) for b in a_bd]
    x2 = [_mm(v, v).astype(BF16) for v in x1]
    x4 = [_mm(v, v).astype(BF16) for v in x2]
    d = [di + _mm(di.astype(BF16), v) for di, v in zip(d, x2)]
    x8 = [_mm(v, v).astype(BF16) for v in x4]
    d = [di + _mm(di.astype(BF16), v) for di, v in zip(d, x4)]
    d = [(di + _mm(di.astype(BF16), v)).astype(BF16) for di, v in zip(d, x8)]
    n = [_mm(di, ao) for di, ao in zip(d, a_off)]
    nb = [v.astype(BF16) for v in n]
    n2 = [_mm(v, v).astype(BF16) for v in nb]
    q = [eye - v for v in n]
    q = [(qi + _mm(qi.astype(BF16), v)).astype(BF16) for qi, v in zip(q, n2)]
    return [_mm(qi, di) for qi, di in zip(q, d)]


def _scan_kernel(qkv_f, lf_f, gdn_f, sm_f, smt_f, qkv_b, lf_b, gdn_b, sm_b, smt_b,
                 o_f, o_b, s_gla, s_gdn):
    @pl.when(pl.program_id(1) == 0)
    def _():
        s_gla[...] = jnp.zeros_like(s_gla)
        s_gdn[...] = jnp.zeros_like(s_gdn)

    n_chunks = qkv_f.shape[1] // CHUNK
    ri = lax.broadcasted_iota(jnp.int32, (CHUNK, CHUNK), 0)
    ci = lax.broadcasted_iota(jnp.int32, (CHUNK, CHUNK), 1)
    lower = ri >= ci
    upper = ri <= ci
    eye = jnp.where(ri == ci, 1.0, 0.0).astype(F32)
    bd_mask = (ri // SUB) == (ci // SUB)
    tri_lo = jnp.where(lower, 1.0, 0.0).astype(BF16)
    tri_up = jnp.where(upper, 1.0, 0.0).astype(BF16)
    ones_bc = jnp.ones((CHUNK, GLA_DV), BF16)
    tn = (((0,), (0,)), ((), ()))

    refs = ((qkv_f, lf_f, gdn_f, sm_f, smt_f, o_f), (qkv_b, lf_b, gdn_b, sm_b, smt_b, o_b))
    consts = ((tri_lo, tri_up, lower, ri > ci, CHUNK - 1), (tri_up, tri_lo, upper, ri < ci, 0))

    units = []
    for step in range(n_chunks):
        for d in range(2):
            qkv, lf, gdn, sm, smt, _ = refs[d]
            tri, tri_t, m_incl, m_strict, last = consts[d]
            c = step if d == 0 else n_chunks - 1 - step
            rows = slice(c * CHUNK, (c + 1) * CHUNK)
            x = qkv[0, rows, :].astype(F32)
            lf_hi, lf_lo = _split(lf[0, rows, :])
            small = sm[0, rows, :]
            sm_hi, sm_lo = _split(small)
            smt_hi, smt_lo = _split(smt[0, :, rows])
            units.append(dict(
                d=d, rows=rows, q=x[:, :GLA_QK], k=x[:, GLA_QK:2 * GLA_QK], v=x[:, 2 * GLA_QK:],
                y=gdn[0, rows, :].astype(F32), small=small,
                cum=_mm(tri, lf_hi) + _mm(tri, lf_lo),
                total=(lax.dot_general(lf_hi, ones_bc, tn, preferred_element_type=F32)
                       + lax.dot_general(lf_lo, ones_bc, tn, preferred_element_type=F32)),
                g_cols=_mm(tri, sm_hi) + _mm(tri, sm_lo),
                g_rows=_mm(smt_hi, tri_t) + _mm(smt_lo, tri_t)))

    first_of_pair = lax.broadcasted_iota(jnp.int32, (CHUNK, 2 * GLA_DK), 1) < GLA_DK
    for u in units:
        _, _, m_incl, _, last = consts[u["d"]]
        cum = u["cum"]
        q_in = u["q"] * jnp.exp(cum)
        k_in = u["k"] * jnp.exp(-cum)
        k_out = u["k"] * jnp.exp(cum[last:last + 1] - cum)
        q_sel, att, kv = [], [], []
        for pair in range(GLA_HEADS // 2):
            ps = slice(pair * 2 * GLA_DK, (pair + 1) * 2 * GLA_DK)
            vs = slice(pair * 2 * GLA_DV, (pair + 1) * 2 * GLA_DV)
            q_pair = [jnp.where(first_of_pair, q_in[:, ps], 0.0), jnp.where(first_of_pair, 0.0, q_in[:, ps])]
            k_pair = [jnp.where(first_of_pair, k_out[:, ps], 0.0), jnp.where(first_of_pair, 0.0, k_out[:, ps])]
            scores = _dot_nt(jnp.concatenate(q_pair, axis=0), k_in[:, ps])
            q_sel += q_pair
            att += [jnp.where(m_incl, scores[:CHUNK], 0.0), jnp.where(m_incl, scores[CHUNK:], 0.0)]
            v_pair = jnp.concatenate([u["v"][:, vs][:, :GLA_DV], u["v"][:, vs][:, GLA_DV:]], axis=0)
            kv.append(_dot_tn(jnp.concatenate(k_pair, axis=0), v_pair))
        u["q_sel"] = q_sel
        u["att"] = att
        u["kv"] = jnp.concatenate(kv, axis=0)

    chains = []
    for u in units:
        d = u["d"]
        _, _, m_incl, m_strict, last = consts[d]
        y = u["y"]
        for h in range(GDN_HEADS):
            q = y[:, h * GDN_DK:(h + 1) * GDN_DK]
            k = y[:, GDN_QK + h * GDN_DK:GDN_QK + (h + 1) * GDN_DK]
            v = y[:, 2 * GDN_QK + h * GDN_DV:2 * GDN_QK + (h + 1) * GDN_DV]
            lane = SM_GDEC + GDN_HEADS * d + h
            lane_b = SM_BETA + GDN_HEADS * d + h
            r = GDN_HEADS * d + h
            g_bc = jnp.broadcast_to(u["g_cols"][:, lane:lane + 1], (CHUNK, GDN_DK))
            b_bc = jnp.broadcast_to(u["small"][:, lane_b:lane_b + 1], (CHUNK, GDN_DK))
            g_row = u["g_rows"][r:r + 1, :]
            decay = jnp.where(m_incl, jnp.exp(g_bc[:, :CHUNK] - g_row), 0.0)
            e_g = jnp.exp(g_bc)
            g_last = g_row[:, last:last + 1]
            p = _dot_nt(jnp.concatenate([q, k], axis=0), k)
            chains.append(dict(
                u=u, h=h, qk=p[:CHUNK] * decay,
                a=jnp.where(m_strict, p[CHUNK:] * b_bc[:, :CHUNK] * decay, 0.0),
                rhs=jnp.concatenate([v * b_bc, k * (b_bc * e_g)], axis=1),
                q_dec=q * e_g, k_dec=k * jnp.exp(g_last - g_bc), e_last=jnp.exp(g_last)))
    t_list = _inv_unit_triangular([c["a"] for c in chains], bd_mask, eye)
    for c, t in zip(chains, t_list):
        c["uw"] = _dot(t, c["rhs"])

    sg = [s_gla[0], s_gla[1]]
    sd = [[s_gdn[d, h] for h in range(GDN_HEADS)] for d in range(2)]
    for step in range(n_chunks):
        for d in range(2):
            u = units[2 * step + d]
            o_ref = refs[d][5]
            s = sg[d]
            for h in range(GLA_HEADS):
                ps = slice((h // 2) * 2 * GLA_DK, (h // 2 + 1) * 2 * GLA_DK)
                vs = slice(h * GLA_DV, (h + 1) * GLA_DV)
                lhs = jnp.concatenate([u["q_sel"][h], u["att"][h]], axis=1)
                rhs = jnp.concatenate([s[ps], u["v"][:, vs]], axis=0)
                o_ref[0, u["rows"], vs] = _dot(lhs, rhs)
            sg[d] = jnp.exp(u["total"]) * s + u["kv"]
        cs = chains[2 * GDN_HEADS * step:2 * GDN_HEADS * (step + 1)]
        r = [_dot(jnp.concatenate([c["uw"][:, GDN_DV:], c["q_dec"]], axis=0), sd[c["u"]["d"]][c["h"]])
             for c in cs]
        v_new = [c["uw"][:, :GDN_DV] - ri_[:CHUNK] for c, ri_ in zip(cs, r)]
        for c, ri_, vn in zip(cs, r, v_new):
            d, h = c["u"]["d"], c["h"]
            o_ref = refs[d][5]
            o_ref[0, c["u"]["rows"], GLA_V + h * GDN_DV:GLA_V + (h + 1) * GDN_DV] = (
                ri_[CHUNK:] + _dot(c["qk"], vn))
            sd[d][h] = c["e_last"] * sd[d][h] + _dot_tn(c["k_dec"], vn)
    for d in range(2):
        s_gla[d] = sg[d]
        for h in range(GDN_HEADS):
            s_gdn[d, h] = sd[d][h]


def _scan(qkv, lff, lfb, gdn, small, small_t, tb):
    bsz, seq, _ = qkv.shape
    nb = seq // tb
    fwd = lambda w: pl.BlockSpec((1, tb, w), lambda b, j: (b, j, 0))
    bwd = lambda w: pl.BlockSpec((1, tb, w), lambda b, j: (b, nb - 1 - j, 0))
    rows = 4 * GDN_HEADS
    in_specs = [fwd(qkv.shape[2]), fwd(GLA_QK), fwd(GDN_CONV_W), fwd(SM_W),
                pl.BlockSpec((1, rows, tb), lambda b, j: (b, 0, j)),
                bwd(qkv.shape[2]), bwd(GLA_QK), bwd(GDN_CONV_W), bwd(SM_W),
                pl.BlockSpec((1, rows, tb), lambda b, j: (b, 0, nb - 1 - j))]
    width = GLA_V + GDN_V
    return pl.pallas_call(
        _scan_kernel,
        grid=(bsz, nb),
        in_specs=in_specs,
        out_specs=(fwd(width), bwd(width)),
        out_shape=(jax.ShapeDtypeStruct((bsz, seq, width), F32),) * 2,
        scratch_shapes=[pltpu.VMEM((2, GLA_QK, GLA_DV), F32),
                        pltpu.VMEM((2, GDN_HEADS, GDN_DK, GDN_DV), F32)],
        compiler_params=pltpu.CompilerParams(
            dimension_semantics=("arbitrary", "arbitrary"), vmem_limit_bytes=VMEM_LIMIT),
        name="scan",
    )(qkv, lff, gdn, small, small_t, qkv, lfb, gdn, small, small_t)


def _outproj_kernel(of_ref, ob_ref, gates_ref, x_ref, gn_ref, wout_ref, x1_ref):
    o = of_ref[0] + ob_ref[0]
    g = gates_ref[0].astype(F32)
    gn = gn_ref[...]
    parts = []
    for h in range(GLA_HEADS + GDN_HEADS):
        hs = slice(h * GLA_DV, (h + 1) * GLA_DV)
        gate = g[:, hs]
        parts.append(_rms(o[:, hs], gn[:, hs]) * (gate * _sigmoid(gate)))
    y = jnp.concatenate(parts, axis=-1).astype(BF16)
    x1_ref[0] = x_ref[0] + jnp.dot(y, wout_ref[...], preferred_element_type=F32)


def _outproj(o_f, o_b, gates, x, gn, wout, tm):
    bsz, seq, _ = x.shape
    tok = pl.BlockSpec((1, tm, D_MODEL), lambda b, j: (b, j, 0))
    return pl.pallas_call(
        _outproj_kernel,
        grid=(bsz, seq // tm),
        in_specs=[tok, tok, tok, tok, _const_spec(gn.shape), _const_spec(wout.shape)],
        out_specs=tok,
        out_shape=jax.ShapeDtypeStruct((bsz, seq, D_MODEL), F32),
        compiler_params=pltpu.CompilerParams(
            dimension_semantics=("arbitrary", "arbitrary"), vmem_limit_bytes=VMEM_LIMIT),
        name="outproj",
    )(o_f, o_b, gates, x, gn, wout)


def _ffn_kernel(x_ref, xp_ref, xn_ref, nw_ref, wup_ref, conv_ref, wdown_ref, fw_ref, out_ref):
    j = pl.program_id(1)
    nj = pl.num_programs(1)
    tm = x_ref.shape[1]
    x = x_ref[0]
    xe = jnp.concatenate([x, xp_ref[0], xn_ref[0]], axis=0)
    he = _rms(xe, nw_ref[...]).astype(BF16)
    row = lax.broadcasted_iota(jnp.int32, (tm, 2 * FF_TILE), 0)
    first = row == 0
    final = row == tm - 1
    n_tiles = D_FF // FF_TILE

    def up(t):
        return jnp.dot(he, wup_ref[t], preferred_element_type=F32)

    ue = up(0)
    acc = None
    for t in range(n_tiles):
        ue_ahead = up(t + 1) if t + 1 < n_tiles else None
        u = ue[:tm]
        u_before = jnp.where(j == 0, 0.0, ue[tm + HALO - 1:tm + HALO])
        u_after = jnp.where(j == nj - 1, 0.0, ue[tm + HALO:tm + HALO + 1])
        u_prev = jnp.where(first, u_before, pltpu.roll(u, 1, 0))
        u_next = jnp.where(final, u_after, pltpu.roll(u, tm - 1, 0))
        cw = conv_ref[t]
        cu = u_prev * cw[0:1] + u * cw[1:2] + u_next * cw[2:3]
        gate = cu[:, FF_TILE:]
        act = (gate * _sigmoid(gate) * cu[:, :FF_TILE]).astype(BF16)
        part = jnp.dot(act, wdown_ref[t], preferred_element_type=F32)
        acc = part if acc is None else acc + part
        ue = ue_ahead
    out_ref[0] = _rms(x + acc, fw_ref[...])


def _ffn(x1, nw, wup, conv, wdown, fw, tm):
    bsz, seq, _ = x1.shape
    prev, nxt = _halo_specs(tm, seq, D_MODEL)
    tok = pl.BlockSpec((1, tm, D_MODEL), lambda b, j: (b, j, 0))
    return pl.pallas_call(
        _ffn_kernel,
        grid=(bsz, seq // tm),
        in_specs=[tok, prev, nxt] + [_const_spec(a.shape) for a in (nw, wup, conv, wdown, fw)],
        out_specs=tok,
        out_shape=jax.ShapeDtypeStruct((bsz, seq, D_MODEL), F32),
        compiler_params=pltpu.CompilerParams(
            dimension_semantics=("arbitrary", "arbitrary"), vmem_limit_bytes=VMEM_LIMIT),
        name="ffn",
    )(x1, x1, x1, nw, wup, conv, wdown, fw)


def _stage_params(attn_norm, w_in, gla_gate_w, gla_gate_b, gla_norm, gdn_conv, gdn_a_log, gdn_dt_bias,
                  gdn_norm, w_out, ffn_norm, w_up, ffn_conv, w_down, final_norm):
    w = w_in[0]
    o_gg = 2 * GLA_QK + GLA_V
    o_lr = o_gg + GLA_V
    o_dq = o_lr + 2 * GLA_RANK
    o_dg = o_dq + GDN_CONV_W
    o_da = o_dg + GDN_V
    wa = w[:, :o_gg].astype(BF16)
    wg = jnp.concatenate([w[:, o_gg:o_lr], w[:, o_dg:o_da]], axis=1).astype(BF16)
    wc = w[:, o_dq:o_dg].astype(BF16)
    ws = jnp.concatenate([w[:, o_lr:o_dq], w[:, o_da:],
                          jnp.zeros((D_MODEL, SM_W - 2 * GLA_RANK - 4 * GDN_HEADS), F32)], axis=1).astype(BF16)
    gw = jnp.zeros((SM_W, 2 * GLA_QK), F32)
    gw = gw.at[:GLA_RANK, :GLA_QK].set(gla_gate_w[0, 0])
    gw = gw.at[GLA_RANK:2 * GLA_RANK, GLA_QK:].set(gla_gate_w[0, 1]).astype(BF16)
    gb = gla_gate_b[0].reshape(1, 2 * GLA_QK)
    pad8 = lambda a: jnp.concatenate([a, jnp.zeros((HALO - a.shape[0],) + a.shape[1:], a.dtype)], axis=0)
    conv = pad8(gdn_conv[0])
    lanes = jnp.zeros((1, SM_W), F32)
    alog = lanes.at[0, SM_GDEC:SM_BETA].set(gdn_a_log[0].reshape(-1))
    dtb = lanes.at[0, SM_GDEC:SM_BETA].set(gdn_dt_bias[0].reshape(-1))
    gn = jnp.concatenate([jnp.tile(gla_norm[0], GLA_HEADS), jnp.tile(gdn_norm[0], GDN_HEADS)]).reshape(1, -1)
    nt = D_FF // FF_TILE
    up = w_up[0].reshape(D_MODEL, 2, nt, FF_TILE).transpose(2, 0, 1, 3).reshape(nt, D_MODEL, 2 * FF_TILE)
    fconv = ffn_conv[0].reshape(3, 2, nt, FF_TILE).transpose(2, 0, 1, 3).reshape(nt, 3, 2 * FF_TILE)
    fconv = jnp.concatenate([fconv, jnp.zeros((nt, HALO - 3, 2 * FF_TILE), F32)], axis=1)
    down = w_down[0].reshape(nt, FF_TILE, D_MODEL)
    return dict(
        attn_norm=attn_norm[0].reshape(1, -1), wa=wa, wg=wg, wc=wc, ws=ws, gw=gw, gb=gb, conv=conv,
        alog=alog, dtb=dtb, gn=gn, wout=w_out[0].astype(BF16), ffn_norm=ffn_norm[0].reshape(1, -1),
        wup=up.astype(BF16), fconv=fconv, wdown=down.astype(BF16), final_norm=final_norm.reshape(1, -1))


def _tile(seq, want):
    t = min(want, seq)
    assert seq % t == 0 and t % 16 == 0
    return t


def _encode(x, p):
    seq = x.shape[1]
    tm = _tile(seq, 512)
    tb = _tile(seq, 256)
    qkv, lff, lfb, gdn, gates, small, small_t = _inproj(
        x, p["attn_norm"], p["wa"], p["wg"], p["wc"], p["ws"], p["gw"], p["gb"], p["conv"],
        p["alog"], p["dtb"], tm)
    o_f, o_b = _scan(qkv, lff, lfb, gdn, small, small_t, tb)
    x1 = _outproj(o_f, o_b, gates, x, p["gn"], p["wout"], tm)
    return _ffn(x1, p["ffn_norm"], p["wup"], p["fconv"], p["wdown"], p["final_norm"], tm)


def kernel(x_prompt, x_sample, attn_norm, w_in, gla_gate_w, gla_gate_b, gla_norm, gdn_conv, gdn_a_log,
           gdn_dt_bias, gdn_norm, w_out, ffn_norm, w_up, ffn_conv, w_down, final_norm):
    p = _stage_params(attn_norm, w_in, gla_gate_w, gla_gate_b, gla_norm, gdn_conv, gdn_a_log, gdn_dt_bias,
                      gdn_norm, w_out, ffn_norm, w_up, ffn_conv, w_down, final_norm)
    return (_encode(x_prompt, p), _encode(x_sample, p))
```

```python
import jax
import jax.numpy as jnp
from jax import lax
from jax.experimental import pallas as pl
from jax.experimental.pallas import tpu as pltpu

D_MODEL = 1024
CHUNK = 64
GLA_HEADS = 4
GLA_DK = 64
GLA_DV = 128
GLA_RANK = 16
GLA_GATE_NORM = 16.0
GDN_HEADS = 4
GDN_DK = 128
GDN_DV = 128
D_FF = 2816
EPS = 1e-6

GLA_QK = GLA_HEADS * GLA_DK
GLA_V = GLA_HEADS * GLA_DV
GDN_QK = GDN_HEADS * GDN_DK
GDN_V = GDN_HEADS * GDN_DV
GDN_CONV_W = 2 * GDN_QK + GDN_V

SM_W = 128
SM_GDEC = 2 * GLA_RANK
SM_BETA = SM_GDEC + 2 * GDN_HEADS

HALO = 8
FF_TILE = 256
SUB = 16

VMEM_LIMIT = 56 * 1024 * 1024

F32 = jnp.float32
BF16 = jnp.bfloat16


def _dot(a, b):
    return jnp.dot(a.astype(BF16), b.astype(BF16), preferred_element_type=F32)


def _dot_nt(a, b):
    return lax.dot_general(a.astype(BF16), b.astype(BF16), (((1,), (1,)), ((), ())),
                           preferred_element_type=F32)


def _dot_tn(a, b):
    return lax.dot_general(a.astype(BF16), b.astype(BF16), (((0,), (0,)), ((), ())),
                           preferred_element_type=F32)


def _split(a):
    hi = a.astype(BF16)
    lo = (a - hi.astype(F32)).astype(BF16)
    return hi, lo


def _sigmoid(x):
    return 1.0 / (1.0 + jnp.exp(-x))


def _softplus(x):
    return jnp.maximum(x, 0.0) + jnp.log1p(jnp.exp(-jnp.abs(x)))


def _rms(x, w):
    return x * lax.rsqrt(jnp.mean(x * x, axis=-1, keepdims=True) + EPS) * w


def _inproj_kernel(x_ref, xp_ref, xn_ref, nw_ref, wa_ref, wg_ref, wc_ref, ws_ref, gw_ref, gb_ref,
                   conv_ref, alog_ref, dtb_ref,
                   qkv_ref, lff_ref, lfb_ref, gdn_ref, gates_ref, small_ref, smallt_ref):
    j = pl.program_id(1)
    nj = pl.num_programs(1)
    tm = x_ref.shape[1]

    xe = jnp.concatenate([x_ref[0], xp_ref[0], xn_ref[0]], axis=0)
    he = _rms(xe, nw_ref[...]).astype(BF16)
    h = he[:tm]

    row = lax.broadcasted_iota(jnp.int32, (tm, GDN_QK), 0)
    first_row = row == 0
    last_row = row == tm - 1

    def conv_group(g):
        cols = slice(g * GDN_QK, (g + 1) * GDN_QK)
        ce = jnp.dot(he, wc_ref[:, cols], preferred_element_type=F32)
        c = ce[:tm]
        c_before = jnp.where(j == 0, 0.0, ce[tm + HALO - 1:tm + HALO])
        c_after = jnp.where(j == nj - 1, 0.0, ce[tm + HALO:tm + HALO + 1])
        c_prev = jnp.where(first_row, c_before, pltpu.roll(c, 1, 0))
        c_next = jnp.where(last_row, c_after, pltpu.roll(c, tm - 1, 0))
        cw = conv_ref[:, cols]
        y = c_prev * cw[0:1] + c * cw[1:2] + c_next * cw[2:3]
        return y * _sigmoid(y)

    def store_normalised(g, y, scale):
        for i in range(GDN_HEADS):
            seg = y[:, i * GDN_DK:(i + 1) * GDN_DK]
            seg = seg * (lax.rsqrt(jnp.sum(seg * seg, axis=-1, keepdims=True) + EPS) * scale)
            gdn_ref[0, :, g * GDN_QK + i * GDN_DK:g * GDN_QK + (i + 1) * GDN_DK] = seg.astype(BF16)

    a = jnp.dot(h, wa_ref[...], preferred_element_type=F32)
    qkv_ref[0, :, :GLA_QK] = (a[:, :GLA_QK] * (GLA_DK ** -0.5)).astype(BF16)
    qkv_ref[0, :, GLA_QK:] = a[:, GLA_QK:].astype(BF16)

    gates_ref[0] = jnp.dot(h, wg_ref[...], preferred_element_type=F32).astype(BF16)

    s = jnp.dot(h, ws_ref[...], preferred_element_type=F32)
    logits = jnp.dot(s.astype(BF16), gw_ref[...], preferred_element_type=F32) + gb_ref[...]
    store_normalised(0, conv_group(0), GDN_DK ** -0.5)
    store_normalised(1, conv_group(1), 1.0)
    gdn_ref[0, :, 2 * GDN_QK:] = conv_group(2).astype(BF16)

    log_f = (jnp.minimum(logits, 0.0) - jnp.log1p(jnp.exp(-jnp.abs(logits)))) * (1.0 / GLA_GATE_NORM)
    lff_ref[0] = log_f[:, :GLA_QK]
    lfb_ref[0] = log_f[:, GLA_QK:]

    lane = lax.broadcasted_iota(jnp.int32, s.shape, 1)
    gdec = -jnp.exp(alog_ref[...]) * _softplus(s + dtb_ref[...])
    beta = _sigmoid(s)
    small = jnp.where((lane >= SM_GDEC) & (lane < SM_BETA), gdec,
                      jnp.where((lane >= SM_BETA) & (lane < SM_BETA + 2 * GDN_HEADS), beta, 0.0))
    small_ref[0] = small
    smallt_ref[0] = jnp.transpose(small)[SM_GDEC:SM_GDEC + 4 * GDN_HEADS]


def _halo_specs(tm, seq, width, rows):
    per = tm // rows
    last = seq // rows - 1
    prev = pl.BlockSpec((1, rows, width), lambda b, j: (b, jnp.maximum(j * per - 1, 0), 0))
    nxt = pl.BlockSpec((1, rows, width), lambda b, j: (b, jnp.minimum((j + 1) * per, last), 0))
    return prev, nxt


def _const_spec(shape):
    return pl.BlockSpec(shape, lambda b, j: (0,) * len(shape))


def _inproj(x, nw, wa, wg, wc, ws, gw, gb, conv, alog, dtb, tm):
    bsz, seq, _ = x.shape
    prev, nxt = _halo_specs(tm, seq, D_MODEL, HALO)
    tok = lambda w: pl.BlockSpec((1, tm, w), lambda b, j: (b, j, 0))
    out_shape = (
        jax.ShapeDtypeStruct((bsz, seq, GLA_QK * 2 + GLA_V), BF16),
        jax.ShapeDtypeStruct((bsz, seq, GLA_QK), F32),
        jax.ShapeDtypeStruct((bsz, seq, GLA_QK), F32),
        jax.ShapeDtypeStruct((bsz, seq, GDN_CONV_W), BF16),
        jax.ShapeDtypeStruct((bsz, seq, GLA_V + GDN_V), BF16),
        jax.ShapeDtypeStruct((bsz, seq, SM_W), F32),
        jax.ShapeDtypeStruct((bsz, 4 * GDN_HEADS, seq), F32),
    )
    out_specs = (tok(GLA_QK * 2 + GLA_V), tok(GLA_QK), tok(GLA_QK), tok(GDN_CONV_W),
                 tok(GLA_V + GDN_V), tok(SM_W),
                 pl.BlockSpec((1, 4 * GDN_HEADS, tm), lambda b, j: (b, 0, j)))
    in_specs = [tok(D_MODEL), prev, nxt] + [_const_spec(a.shape) for a in
                                            (nw, wa, wg, wc, ws, gw, gb, conv, alog, dtb)]
    return pl.pallas_call(
        _inproj_kernel,
        grid=(bsz, seq // tm),
        in_specs=in_specs,
        out_specs=out_specs,
        out_shape=out_shape,
        compiler_params=pltpu.CompilerParams(
            dimension_semantics=("arbitrary", "arbitrary"), vmem_limit_bytes=VMEM_LIMIT),
        name="inproj",
    )(x, x, x, nw, wa, wg, wc, ws, gw, gb, conv, alog, dtb)


def _mm(a, b):
    return jnp.dot(a, b, preferred_element_type=F32)


def _inv_unit_triangular(a_list, bd_mask, eye):
    a_bd = [jnp.where(bd_mask, a, 0.0) for a in a_list]
    a_off = [(a - b).astype(BF16) for a, b in zip(a_list, a_bd)]
    d = [eye - b for b in a_bd]
    x1 = [(-b).astype(BF16) for b in a_bd]
    x2 = [_mm(v, v).astype(BF16) for v in x1]
    x4 = [_mm(v, v).astype(BF16) for v in x2]
    d = [di + _mm(di.astype(BF16), v) for di, v in zip(d, x2)]
    x8 = [_mm(v, v).astype(BF16) for v in x4]
    d = [di + _mm(di.astype(BF16), v) for di, v in zip(d, x4)]
    d = [(di + _mm(di.astype(BF16), v)).astype(BF16) for di, v in zip(d, x8)]
    n = [_mm(di, ao) for di, ao in zip(d, a_off)]
    nb = [v.astype(BF16) for v in n]
    n2 = [_mm(v, v).astype(BF16) for v in nb]
    q = [eye - v for v in n]
    q = [(qi + _mm(qi.astype(BF16), v)).astype(BF16) for qi, v in zip(q, n2)]
    return [_mm(qi, di) for qi, di in zip(q, d)]


def _scan_kernel(qkv_f, lf_f, gdn_f, sm_f, smt_f, qkv_b, lf_b, gdn_b, sm_b, smt_b,
                 o_f, o_b, s_gla, s_gdn):
    @pl.when(pl.program_id(1) == 0)
    def _():
        s_gla[...] = jnp.zeros_like(s_gla)
        s_gdn[...] = jnp.zeros_like(s_gdn)

    n_chunks = qkv_f.shape[1] // CHUNK
    ri = lax.broadcasted_iota(jnp.int32, (CHUNK, CHUNK), 0)
    ci = lax.broadcasted_iota(jnp.int32, (CHUNK, CHUNK), 1)
    lower = ri >= ci
    upper = ri <= ci
    eye = jnp.where(ri == ci, 1.0, 0.0).astype(F32)
    bd_mask = (ri // SUB) == (ci // SUB)
    tri_lo = jnp.where(lower, 1.0, 0.0).astype(BF16)
    tri_up = jnp.where(upper, 1.0, 0.0).astype(BF16)
    ones_bc = jnp.ones((CHUNK, GLA_DV), BF16)
    tn = (((0,), (0,)), ((), ()))

    refs = ((qkv_f, lf_f, gdn_f, sm_f, smt_f, o_f), (qkv_b, lf_b, gdn_b, sm_b, smt_b, o_b))
    consts = ((tri_lo, tri_up, lower, ri > ci, CHUNK - 1), (tri_up, tri_lo, upper, ri < ci, 0))

    units = []
    for step in range(n_chunks):
        for d in range(2):
            qkv, lf, gdn, sm, smt, _ = refs[d]
            tri, tri_t, m_incl, m_strict, last = consts[d]
            c = step if d == 0 else n_chunks - 1 - step
            rows = slice(c * CHUNK, (c + 1) * CHUNK)
            x = qkv[0, rows, :].astype(F32)
            lf_hi, lf_lo = _split(lf[0, rows, :])
            small = sm[0, rows, :]
            sm_hi, sm_lo = _split(small)
            smt_hi, smt_lo = _split(smt[0, :, rows])
            units.append(dict(
                d=d, rows=rows, q=x[:, :GLA_QK], k=x[:, GLA_QK:2 * GLA_QK], v=x[:, 2 * GLA_QK:],
                y=gdn[0, rows, :].astype(F32), small=small,
                cum=_mm(tri, lf_hi) + _mm(tri, lf_lo),
                total=(lax.dot_general(lf_hi, ones_bc, tn, preferred_element_type=F32)
                       + lax.dot_general(lf_lo, ones_bc, tn, preferred_element_type=F32)),
                g_cols=_mm(tri, sm_hi) + _mm(tri, sm_lo),
                g_rows=_mm(smt_hi, tri_t) + _mm(smt_lo, tri_t)))

    first_of_pair = lax.broadcasted_iota(jnp.int32, (CHUNK, 2 * GLA_DK), 1) < GLA_DK
    for u in units:
        _, _, m_incl, _, last = consts[u["d"]]
        cum = u["cum"]
        q_in = u["q"] * jnp.exp(cum)
        k_in = u["k"] * jnp.exp(-cum)
        k_out = u["k"] * jnp.exp(cum[last:last + 1] - cum)
        q_sel, att, kv = [], [], []
        for pair in range(GLA_HEADS // 2):
            ps = slice(pair * 2 * GLA_DK, (pair + 1) * 2 * GLA_DK)
            vs = slice(pair * 2 * GLA_DV, (pair + 1) * 2 * GLA_DV)
            q_pair = [jnp.where(first_of_pair, q_in[:, ps], 0.0), jnp.where(first_of_pair, 0.0, q_in[:, ps])]
            k_pair = [jnp.where(first_of_pair, k_out[:, ps], 0.0), jnp.where(first_of_pair, 0.0, k_out[:, ps])]
            scores = _dot_nt(jnp.concatenate(q_pair, axis=0), k_in[:, ps])
            q_sel += q_pair
            att += [jnp.where(m_incl, scores[:CHUNK], 0.0), jnp.where(m_incl, scores[CHUNK:], 0.0)]
            v_pair = jnp.concatenate([u["v"][:, vs][:, :GLA_DV], u["v"][:, vs][:, GLA_DV:]], axis=0)
            kv.append(_dot_tn(jnp.concatenate(k_pair, axis=0), v_pair))
        u["q_sel"] = q_sel
        u["att"] = att
        u["kv"] = jnp.concatenate(kv, axis=0)

    chains = []
    for u in units:
        d = u["d"]
        _, _, m_incl, m_strict, last = consts[d]
        y = u["y"]
        for h in range(GDN_HEADS):
            q = y[:, h * GDN_DK:(h + 1) * GDN_DK]
            k = y[:, GDN_QK + h * GDN_DK:GDN_QK + (h + 1) * GDN_DK]
            v = y[:, 2 * GDN_QK + h * GDN_DV:2 * GDN_QK + (h + 1) * GDN_DV]
            lane = SM_GDEC + GDN_HEADS * d + h
            lane_b = SM_BETA + GDN_HEADS * d + h
            r = GDN_HEADS * d + h
            g_bc = jnp.broadcast_to(u["g_cols"][:, lane:lane + 1], (CHUNK, GDN_DK))
            b_bc = jnp.broadcast_to(u["small"][:, lane_b:lane_b + 1], (CHUNK, GDN_DK))
            g_row = u["g_rows"][r:r + 1, :]
            decay = jnp.where(m_incl, jnp.exp(g_bc[:, :CHUNK] - g_row), 0.0)
            e_g = jnp.exp(g_bc)
            g_last = g_row[:, last:last + 1]
            p = _dot_nt(jnp.concatenate([q, k], axis=0), k)
            chains.append(dict(
                u=u, h=h, qk=p[:CHUNK] * decay,
                a=jnp.where(m_strict, p[CHUNK:] * b_bc[:, :CHUNK] * decay, 0.0),
                rhs=jnp.concatenate([v * b_bc, k * (b_bc * e_g)], axis=1),
                q_dec=q * e_g, k_dec=k * jnp.exp(g_last - g_bc), e_last=jnp.exp(g_last)))
    t_list = _inv_unit_triangular([c["a"] for c in chains], bd_mask, eye)
    for c, t in zip(chains, t_list):
        c["uw"] = _dot(t, c["rhs"])

    sg = [s_gla[0], s_gla[1]]
    sd = [[s_gdn[d, h] for h in range(GDN_HEADS)] for d in range(2)]
    for step in range(n_chunks):
        for d in range(2):
            u = units[2 * step + d]
            o_ref = refs[d][5]
            s = sg[d]
            for h in range(GLA_HEADS):
                ps = slice((h // 2) * 2 * GLA_DK, (h // 2 + 1) * 2 * GLA_DK)
                vs = slice(h * GLA_DV, (h + 1) * GLA_DV)
                lhs = jnp.concatenate([u["q_sel"][h], u["att"][h]], axis=1)
                rhs = jnp.concatenate([s[ps], u["v"][:, vs]], axis=0)
                o_ref[0, u["rows"], vs] = _dot(lhs, rhs)
            sg[d] = jnp.exp(u["total"]) * s + u["kv"]
        cs = chains[2 * GDN_HEADS * step:2 * GDN_HEADS * (step + 1)]
        r = [_dot(jnp.concatenate([c["uw"][:, GDN_DV:], c["q_dec"]], axis=0), sd[c["u"]["d"]][c["h"]])
             for c in cs]
        v_new = [c["uw"][:, :GDN_DV] - ri_[:CHUNK] for c, ri_ in zip(cs, r)]
        for c, ri_, vn in zip(cs, r, v_new):
            d, h = c["u"]["d"], c["h"]
            o_ref = refs[d][5]
            o_ref[0, c["u"]["rows"], GLA_V + h * GDN_DV:GLA_V + (h + 1) * GDN_DV] = (
                ri_[CHUNK:] + _dot(c["qk"], vn))
            sd[d][h] = c["e_last"] * sd[d][h] + _dot_tn(c["k_dec"], vn)
    for d in range(2):
        s_gla[d] = sg[d]
        for h in range(GDN_HEADS):
            s_gdn[d, h] = sd[d][h]


def _scan(qkv, lff, lfb, gdn, small, small_t, tb):
    bsz, seq, _ = qkv.shape
    nb = seq // tb
    fwd = lambda w: pl.BlockSpec((1, tb, w), lambda b, j: (b, j, 0))
    bwd = lambda w: pl.BlockSpec((1, tb, w), lambda b, j: (b, nb - 1 - j, 0))
    rows = 4 * GDN_HEADS
    in_specs = [fwd(qkv.shape[2]), fwd(GLA_QK), fwd(GDN_CONV_W), fwd(SM_W),
                pl.BlockSpec((1, rows, tb), lambda b, j: (b, 0, j)),
                bwd(qkv.shape[2]), bwd(GLA_QK), bwd(GDN_CONV_W), bwd(SM_W),
                pl.BlockSpec((1, rows, tb), lambda b, j: (b, 0, nb - 1 - j))]
    width = GLA_V + GDN_V
    return pl.pallas_call(
        _scan_kernel,
        grid=(bsz, nb),
        in_specs=in_specs,
        out_specs=(fwd(width), bwd(width)),
        out_shape=(jax.ShapeDtypeStruct((bsz, seq, width), F32),) * 2,
        scratch_shapes=[pltpu.VMEM((2, GLA_QK, GLA_DV), F32),
                        pltpu.VMEM((2, GDN_HEADS, GDN_DK, GDN_DV), F32)],
        compiler_params=pltpu.CompilerParams(
            dimension_semantics=("arbitrary", "arbitrary"), vmem_limit_bytes=VMEM_LIMIT),
        name="scan",
    )(qkv, lff, gdn, small, small_t, qkv, lfb, gdn, small, small_t)


def _mix_ffn_kernel(of_ref, ofp_ref, ofn_ref, ob_ref, obp_ref, obn_ref, g_ref, gp_ref, gx_ref,
                    x_ref, xp_ref, xn_ref, gn_ref, wout_ref, nw_ref, wup_ref, conv_ref, wdown_ref,
                    fw_ref, out_ref):
    j = pl.program_id(1)
    nj = pl.num_programs(1)
    tm = x_ref.shape[1]
    half = tm // 2
    ext = half + 2 * HALO
    n_tiles = D_FF // FF_TILE
    gn = gn_ref[...]
    row = lax.broadcasted_iota(jnp.int32, (ext, D_MODEL), 0)
    pad_row = ((row == HALO - 1) & (j == 0), (row == HALO + half) & (j == nj - 1))

    def frame(idx, main_ref, prev, nxt):
        if idx == 0:
            return jnp.concatenate([prev, main_ref[0, :half + HALO, :]], axis=0)
        return jnp.concatenate([main_ref[0, half - HALO:, :], nxt], axis=0)

    def gate_frame(idx):
        if idx == 0:
            main = g_ref[0, :half + 2 * HALO, :].astype(F32)[:half + HALO]
            return jnp.concatenate([gp_ref[0].astype(F32)[HALO:], main], axis=0)
        main = g_ref[0, half - 2 * HALO:, :].astype(F32)[HALO:]
        return jnp.concatenate([main, gx_ref[0].astype(F32)[:HALO]], axis=0)

    state = {}

    def prologue(idx):
        o = (frame(idx, of_ref, ofp_ref[0], ofn_ref[0])
             + frame(idx, ob_ref, obp_ref[0], obn_ref[0]))
        g = gate_frame(idx)
        parts = []
        for h in range(GLA_HEADS + GDN_HEADS):
            hs = slice(h * GLA_DV, (h + 1) * GLA_DV)
            gate = g[:, hs]
            parts.append(_rms(o[:, hs], gn[:, hs]) * (gate * _sigmoid(gate)))
        y = jnp.concatenate(parts, axis=-1).astype(BF16)
        x1 = (frame(idx, x_ref, xp_ref[0], xn_ref[0])
              + jnp.dot(y, wout_ref[...], preferred_element_type=F32))
        he = jnp.where(pad_row[idx], 0.0, _rms(x1, nw_ref[...])).astype(BF16)
        state[idx] = (x1[HALO:HALO + half], he)

    def ffn_half(idx, hook):
        x1, he = state[idx]
        up = lambda t: jnp.dot(he, wup_ref[t], preferred_element_type=F32)
        ue = up(0)
        acc = None
        for t in range(n_tiles):
            ue_ahead = up(t + 1) if t + 1 < n_tiles else None
            if t == n_tiles // 2 and hook is not None:
                hook()
            cw = conv_ref[t]
            cu = (pltpu.roll(ue, 1, 0) * cw[0:1] + ue * cw[1:2] + pltpu.roll(ue, ext - 1, 0) * cw[2:3])
            cu = cu[HALO:HALO + half]
            gate = cu[:, FF_TILE:]
            act = (gate * _sigmoid(gate) * cu[:, :FF_TILE]).astype(BF16)
            part = jnp.dot(act, wdown_ref[t], preferred_element_type=F32)
            acc = part if acc is None else acc + part
            ue = ue_ahead
        out_ref[0, idx * half:(idx + 1) * half, :] = _rms(x1 + acc, fw_ref[...])

    prologue(0)
    ffn_half(0, lambda: prologue(1))
    ffn_half(1, None)


def _mix_ffn(o_f, o_b, gates, x, gn, wout, nw, wup, conv, wdown, fw, tm):
    bsz, seq, _ = x.shape
    prev, nxt = _halo_specs(tm, seq, D_MODEL, HALO)
    prev16, nxt16 = _halo_specs(tm, seq, D_MODEL, 2 * HALO)
    tok = pl.BlockSpec((1, tm, D_MODEL), lambda b, j: (b, j, 0))
    once = lambda a: pl.BlockSpec(a.shape, lambda b, j: (0,) * a.ndim, pipeline_mode=pl.Buffered(1))
    in_specs = ([tok, prev, nxt, tok, prev, nxt, tok, prev16, nxt16, tok, prev, nxt]
                + [_const_spec(gn.shape), once(wout), _const_spec(nw.shape), once(wup),
                   _const_spec(conv.shape), once(wdown), _const_spec(fw.shape)])
    return pl.pallas_call(
        _mix_ffn_kernel,
        grid=(bsz, seq // tm),
        in_specs=in_specs,
        out_specs=tok,
        out_shape=jax.ShapeDtypeStruct((bsz, seq, D_MODEL), F32),
        compiler_params=pltpu.CompilerParams(
            dimension_semantics=("arbitrary", "arbitrary"), vmem_limit_bytes=VMEM_LIMIT),
        name="mix_ffn",
    )(o_f, o_f, o_f, o_b, o_b, o_b, gates, gates, gates, x, x, x, gn, wout, nw, wup, conv, wdown, fw)


def _stage_params(attn_norm, w_in, gla_gate_w, gla_gate_b, gla_norm, gdn_conv, gdn_a_log, gdn_dt_bias,
                  gdn_norm, w_out, ffn_norm, w_up, ffn_conv, w_down, final_norm):
    w = w_in[0]
    o_gg = 2 * GLA_QK + GLA_V
    o_lr = o_gg + GLA_V
    o_dq = o_lr + 2 * GLA_RANK
    o_dg = o_dq + GDN_CONV_W
    o_da = o_dg + GDN_V
    wa = w[:, :o_gg].astype(BF16)
    wg = jnp.concatenate([w[:, o_gg:o_lr], w[:, o_dg:o_da]], axis=1).astype(BF16)
    wc = w[:, o_dq:o_dg].astype(BF16)
    ws = jnp.concatenate([w[:, o_lr:o_dq], w[:, o_da:],
                          jnp.zeros((D_MODEL, SM_W - 2 * GLA_RANK - 4 * GDN_HEADS), F32)], axis=1).astype(BF16)
    gw = jnp.zeros((SM_W, 2 * GLA_QK), F32)
    gw = gw.at[:GLA_RANK, :GLA_QK].set(gla_gate_w[0, 0])
    gw = gw.at[GLA_RANK:2 * GLA_RANK, GLA_QK:].set(gla_gate_w[0, 1]).astype(BF16)
    gb = gla_gate_b[0].reshape(1, 2 * GLA_QK)
    pad8 = lambda a: jnp.concatenate([a, jnp.zeros((HALO - a.shape[0],) + a.shape[1:], a.dtype)], axis=0)
    conv = pad8(gdn_conv[0])
    lanes = jnp.zeros((1, SM_W), F32)
    alog = lanes.at[0, SM_GDEC:SM_BETA].set(gdn_a_log[0].reshape(-1))
    dtb = lanes.at[0, SM_GDEC:SM_BETA].set(gdn_dt_bias[0].reshape(-1))
    gn = jnp.concatenate([jnp.tile(gla_norm[0], GLA_HEADS), jnp.tile(gdn_norm[0], GDN_HEADS)]).reshape(1, -1)
    nt = D_FF // FF_TILE
    up = w_up[0].reshape(D_MODEL, 2, nt, FF_TILE).transpose(2, 0, 1, 3).reshape(nt, D_MODEL, 2 * FF_TILE)
    fconv = ffn_conv[0].reshape(3, 2, nt, FF_TILE).transpose(2, 0, 1, 3).reshape(nt, 3, 2 * FF_TILE)
    fconv = jnp.concatenate([fconv, jnp.zeros((nt, HALO - 3, 2 * FF_TILE), F32)], axis=1)
    down = w_down[0].reshape(nt, FF_TILE, D_MODEL)
    return dict(
        attn_norm=attn_norm[0].reshape(1, -1), wa=wa, wg=wg, wc=wc, ws=ws, gw=gw, gb=gb, conv=conv,
        alog=alog, dtb=dtb, gn=gn, wout=w_out[0].astype(BF16), ffn_norm=ffn_norm[0].reshape(1, -1),
        wup=up.astype(BF16), fconv=fconv, wdown=down.astype(BF16), final_norm=final_norm.reshape(1, -1))


def _tile(seq, want):
    t = min(want, seq)
    assert seq % t == 0 and t % 16 == 0
    return t


def _encode(x, p):
    seq = x.shape[1]
    tm = _tile(seq, 512)
    tb = _tile(seq, 256)
    qkv, lff, lfb, gdn, gates, small, small_t = _inproj(
        x, p["attn_norm"], p["wa"], p["wg"], p["wc"], p["ws"], p["gw"], p["gb"], p["conv"],
        p["alog"], p["dtb"], tm)
    o_f, o_b = _scan(qkv, lff, lfb, gdn, small, small_t, tb)
    return _mix_ffn(o_f, o_b, gates, x, p["gn"], p["wout"], p["ffn_norm"], p["wup"], p["fconv"], p["wdown"],
                    p["final_norm"], tm)


def kernel(x_prompt, x_sample, attn_norm, w_in, gla_gate_w, gla_gate_b, gla_norm, gdn_conv, gdn_a_log,
           gdn_dt_bias, gdn_norm, w_out, ffn_norm, w_up, ffn_conv, w_down, final_norm):
    p = _stage_params(attn_norm, w_in, gla_gate_w, gla_gate_b, gla_norm, gdn_conv, gdn_a_log, gdn_dt_bias,
                      gdn_norm, w_out, ffn_norm, w_up, ffn_conv, w_down, final_norm)
    return (_encode(x_prompt, p), _encode(x_sample, p))
```

```python
import jax
import jax.numpy as jnp
from jax import lax
from jax.experimental import pallas as pl
from jax.experimental.pallas import tpu as pltpu

D_MODEL = 1024
CHUNK = 64
GLA_HEADS = 4
GLA_DK = 64
GLA_DV = 128
GLA_RANK = 16
GLA_GATE_NORM = 16.0
GDN_HEADS = 4
GDN_DK = 128
GDN_DV = 128
D_FF = 2816
EPS = 1e-6

GLA_QK = GLA_HEADS * GLA_DK
GLA_V = GLA_HEADS * GLA_DV
GDN_QK = GDN_HEADS * GDN_DK
GDN_V = GDN_HEADS * GDN_DV
GDN_CONV_W = 2 * GDN_QK + GDN_V

SM_W = 128
SM_GDEC = 2 * GLA_RANK
SM_BETA = SM_GDEC + 2 * GDN_HEADS

HALO = 8
FF_TILE = 256
SUB = 16

VMEM_LIMIT = 56 * 1024 * 1024

F32 = jnp.float32
BF16 = jnp.bfloat16


def _dot(a, b):
    return jnp.dot(a.astype(BF16), b.astype(BF16), preferred_element_type=F32)


def _dot_nt(a, b):
    return lax.dot_general(a.astype(BF16), b.astype(BF16), (((1,), (1,)), ((), ())),
                           preferred_element_type=F32)


def _dot_tn(a, b):
    return lax.dot_general(a.astype(BF16), b.astype(BF16), (((0,), (0,)), ((), ())),
                           preferred_element_type=F32)


def _split(a):
    hi = a.astype(BF16)
    lo = (a - hi.astype(F32)).astype(BF16)
    return hi, lo


def _sigmoid(x):
    return 1.0 / (1.0 + jnp.exp(-x))


def _softplus(x):
    return jnp.maximum(x, 0.0) + jnp.log1p(jnp.exp(-jnp.abs(x)))


def _rms(x, w):
    return x * lax.rsqrt(jnp.mean(x * x, axis=-1, keepdims=True) + EPS) * w


def _inproj_kernel(x_ref, xp_ref, xn_ref, nw_ref, wa_ref, wg_ref, wc_ref, ws_ref, gw_ref, gb_ref,
                   conv_ref, alog_ref, dtb_ref,
                   qkv_ref, lff_ref, lfb_ref, gdn_ref, gates_ref, small_ref, smallt_ref):
    j = pl.program_id(1)
    nj = pl.num_programs(1)
    tm = x_ref.shape[1]

    xe = jnp.concatenate([x_ref[0], xp_ref[0], xn_ref[0]], axis=0)
    he = _rms(xe, nw_ref[...]).astype(BF16)
    h = he[:tm]

    row = lax.broadcasted_iota(jnp.int32, (tm, GDN_QK), 0)
    first_row = row == 0
    last_row = row == tm - 1

    def conv_group(g):
        cols = slice(g * GDN_QK, (g + 1) * GDN_QK)
        ce = jnp.dot(he, wc_ref[:, cols], preferred_element_type=F32)
        c = ce[:tm]
        c_before = jnp.where(j == 0, 0.0, ce[tm + HALO - 1:tm + HALO])
        c_after = jnp.where(j == nj - 1, 0.0, ce[tm + HALO:tm + HALO + 1])
        c_prev = jnp.where(first_row, c_before, pltpu.roll(c, 1, 0))
        c_next = jnp.where(last_row, c_after, pltpu.roll(c, tm - 1, 0))
        cw = conv_ref[:, cols]
        y = c_prev * cw[0:1] + c * cw[1:2] + c_next * cw[2:3]
        return y * _sigmoid(y)

    def store_normalised(g, y, scale):
        for i in range(GDN_HEADS):
            seg = y[:, i * GDN_DK:(i + 1) * GDN_DK]
            seg = seg * (lax.rsqrt(jnp.sum(seg * seg, axis=-1, keepdims=True) + EPS) * scale)
            gdn_ref[0, :, g * GDN_QK + i * GDN_DK:g * GDN_QK + (i + 1) * GDN_DK] = seg.astype(BF16)

    a = jnp.dot(h, wa_ref[...], preferred_element_type=F32)
    qkv_ref[0, :, :GLA_QK] = (a[:, :GLA_QK] * (GLA_DK ** -0.5)).astype(BF16)
    qkv_ref[0, :, GLA_QK:] = a[:, GLA_QK:].astype(BF16)

    gates_ref[0] = jnp.dot(h, wg_ref[...], preferred_element_type=F32).astype(BF16)

    s = jnp.dot(h, ws_ref[...], preferred_element_type=F32)
    logits = jnp.dot(s.astype(BF16), gw_ref[...], preferred_element_type=F32) + gb_ref[...]
    store_normalised(0, conv_group(0), GDN_DK ** -0.5)
    store_normalised(1, conv_group(1), 1.0)
    gdn_ref[0, :, 2 * GDN_QK:] = conv_group(2).astype(BF16)

    log_f = (jnp.minimum(logits, 0.0) - jnp.log1p(jnp.exp(-jnp.abs(logits)))) * (1.0 / GLA_GATE_NORM)
    lff_ref[0] = log_f[:, :GLA_QK]
    lfb_ref[0] = log_f[:, GLA_QK:]

    lane = lax.broadcasted_iota(jnp.int32, s.shape, 1)
    gdec = -jnp.exp(alog_ref[...]) * _softplus(s + dtb_ref[...])
    beta = _sigmoid(s)
    small = jnp.where((lane >= SM_GDEC) & (lane < SM_BETA), gdec,
                      jnp.where((lane >= SM_BETA) & (lane < SM_BETA + 2 * GDN_HEADS), beta, 0.0))
    small_ref[0] = small
    smallt_ref[0] = jnp.transpose(small)[SM_GDEC:SM_GDEC + 4 * GDN_HEADS]


def _halo_specs(tm, seq, width, rows):
    per = tm // rows
    last = seq // rows - 1
    prev = pl.BlockSpec((1, rows, width), lambda b, j: (b, jnp.maximum(j * per - 1, 0), 0))
    nxt = pl.BlockSpec((1, rows, width), lambda b, j: (b, jnp.minimum((j + 1) * per, last), 0))
    return prev, nxt


def _const_spec(shape):
    return pl.BlockSpec(shape, lambda b, j: (0,) * len(shape))


def _inproj(x, nw, wa, wg, wc, ws, gw, gb, conv, alog, dtb, tm):
    bsz, seq, _ = x.shape
    prev, nxt = _halo_specs(tm, seq, D_MODEL, HALO)
    tok = lambda w: pl.BlockSpec((1, tm, w), lambda b, j: (b, j, 0))
    out_shape = (
        jax.ShapeDtypeStruct((bsz, seq, GLA_QK * 2 + GLA_V), BF16),
        jax.ShapeDtypeStruct((bsz, seq, GLA_QK), F32),
        jax.ShapeDtypeStruct((bsz, seq, GLA_QK), F32),
        jax.ShapeDtypeStruct((bsz, seq, GDN_CONV_W), BF16),
        jax.ShapeDtypeStruct((bsz, seq, GLA_V + GDN_V), BF16),
        jax.ShapeDtypeStruct((bsz, seq, SM_W), F32),
        jax.ShapeDtypeStruct((bsz, 4 * GDN_HEADS, seq), F32),
    )
    out_specs = (tok(GLA_QK * 2 + GLA_V), tok(GLA_QK), tok(GLA_QK), tok(GDN_CONV_W),
                 tok(GLA_V + GDN_V), tok(SM_W),
                 pl.BlockSpec((1, 4 * GDN_HEADS, tm), lambda b, j: (b, 0, j)))
    in_specs = [tok(D_MODEL), prev, nxt] + [_const_spec(a.shape) for a in
                                            (nw, wa, wg, wc, ws, gw, gb, conv, alog, dtb)]
    return pl.pallas_call(
        _inproj_kernel,
        grid=(bsz, seq // tm),
        in_specs=in_specs,
        out_specs=out_specs,
        out_shape=out_shape,
        compiler_params=pltpu.CompilerParams(
            dimension_semantics=("arbitrary", "arbitrary"), vmem_limit_bytes=VMEM_LIMIT),
        name="inproj",
    )(x, x, x, nw, wa, wg, wc, ws, gw, gb, conv, alog, dtb)


def _mm(a, b):
    return jnp.dot(a, b, preferred_element_type=F32)


def _inv_unit_triangular(a_list, bd_mask, eye, lane_lo):
    def bd(y):
        return jnp.concatenate([jnp.where(lane_lo, y, 0.0).astype(BF16),
                                jnp.where(lane_lo, 0.0, y).astype(BF16)], axis=0)

    a_bd = [jnp.where(bd_mask, a, 0.0) for a in a_list]
    a_off = [bd(a - b) for a, b in zip(a_list, a_bd)]
    d = [eye - b for b in a_bd]
    x1 = [-b for b in a_bd]
    x2 = [_mm(v.astype(BF16), bd(v)) for v in x1]
    both = [_mm(jnp.concatenate([v, di], axis=0).astype(BF16), bd(v)) for v, di in zip(x2, d)]
    x4 = [b[:CHUNK] for b in both]
    d = [di + b[CHUNK:] for di, b in zip(d, both)]
    both = [_mm(jnp.concatenate([v, di], axis=0).astype(BF16), bd(v)) for v, di in zip(x4, d)]
    x8 = [b[:CHUNK] for b in both]
    d = [di + b[CHUNK:] for di, b in zip(d, both)]
    d = [di + _mm(di.astype(BF16), bd(v)) for di, v in zip(d, x8)]
    n = [_mm(di.astype(BF16), ao) for di, ao in zip(d, a_off)]
    n2 = [_mm(v.astype(BF16), bd(v)) for v in n]
    q = [eye - v for v in n]
    q = [qi + _mm(qi.astype(BF16), bd(v)) for qi, v in zip(q, n2)]
    return [_mm(qi.astype(BF16), bd(di)) for qi, di in zip(q, d)]


def _scan_kernel(qkv_f, lf_f, gdn_f, sm_f, smt_f, qkv_b, lf_b, gdn_b, sm_b, smt_b,
                 o_f, o_b, s_gla, s_gdn):
    @pl.when(pl.program_id(1) == 0)
    def _():
        s_gla[...] = jnp.zeros_like(s_gla)
        s_gdn[...] = jnp.zeros_like(s_gdn)

    n_chunks = qkv_f.shape[1] // CHUNK
    ri = lax.broadcasted_iota(jnp.int32, (CHUNK, CHUNK), 0)
    ci = lax.broadcasted_iota(jnp.int32, (CHUNK, CHUNK), 1)
    lower = ri >= ci
    upper = ri <= ci
    ri2 = lax.broadcasted_iota(jnp.int32, (CHUNK, 2 * CHUNK), 0)
    lane2 = lax.broadcasted_iota(jnp.int32, (CHUNK, 2 * CHUNK), 1)
    lane_lo = lane2 < CHUNK
    ci2 = jnp.where(lane_lo, lane2, lane2 - CHUNK)
    eye2 = jnp.where(ri2 == ci2, 1.0, 0.0).astype(F32)
    bd_mask2 = (ri2 // SUB) == (ci2 // SUB)
    tri_lo = jnp.where(lower, 1.0, 0.0).astype(BF16)
    tri_up = jnp.where(upper, 1.0, 0.0).astype(BF16)
    ones_bc = jnp.ones((CHUNK, GLA_DV), BF16)
    tn = (((0,), (0,)), ((), ()))

    refs = ((qkv_f, lf_f, gdn_f, sm_f, smt_f, o_f), (qkv_b, lf_b, gdn_b, sm_b, smt_b, o_b))
    consts = ((tri_lo, tri_up, lower, (ri2 >= ci2, ri2 > ci2), CHUNK - 1),
              (tri_up, tri_lo, upper, (ri2 <= ci2, ri2 < ci2), 0))

    units = []
    for step in range(n_chunks):
        for d in range(2):
            qkv, lf, gdn, sm, smt, _ = refs[d]
            tri, tri_t, _, _, _ = consts[d]
            c = step if d == 0 else n_chunks - 1 - step
            rows = slice(c * CHUNK, (c + 1) * CHUNK)
            x = qkv[0, rows, :].astype(F32)
            lf_hi, lf_lo = _split(lf[0, rows, :])
            small = sm[0, rows, :]
            sm_hi, sm_lo = _split(small)
            smt_hi, smt_lo = _split(smt[0, :, rows])
            units.append(dict(
                d=d, rows=rows, q=x[:, :GLA_QK], k=x[:, GLA_QK:2 * GLA_QK], v=x[:, 2 * GLA_QK:],
                y=gdn[0, rows, :].astype(F32), small=small,
                cum=_mm(tri, lf_hi) + _mm(tri, lf_lo),
                total=(lax.dot_general(lf_hi, ones_bc, tn, preferred_element_type=F32)
                       + lax.dot_general(lf_lo, ones_bc, tn, preferred_element_type=F32)),
                g_cols=_mm(tri, sm_hi) + _mm(tri, sm_lo),
                g_rows=_mm(smt_hi, tri_t) + _mm(smt_lo, tri_t)))

    first_of_pair = lax.broadcasted_iota(jnp.int32, (CHUNK, 2 * GLA_DK), 1) < GLA_DK
    for u in units:
        _, _, m_incl, _, last = consts[u["d"]]
        cum = u["cum"]
        q_in = u["q"] * jnp.exp(cum)
        k_in = u["k"] * jnp.exp(-cum)
        k_out = u["k"] * jnp.exp(cum[last:last + 1] - cum)
        q_sel, att, kv = [], [], []
        for pair in range(GLA_HEADS // 2):
            ps = slice(pair * 2 * GLA_DK, (pair + 1) * 2 * GLA_DK)
            vs = slice(pair * 2 * GLA_DV, (pair + 1) * 2 * GLA_DV)
            q_pair = [jnp.where(first_of_pair, q_in[:, ps], 0.0), jnp.where(first_of_pair, 0.0, q_in[:, ps])]
            k_pair = [jnp.where(first_of_pair, k_out[:, ps], 0.0), jnp.where(first_of_pair, 0.0, k_out[:, ps])]
            scores = _dot_nt(jnp.concatenate(q_pair, axis=0), k_in[:, ps])
            q_sel += q_pair
            att += [jnp.where(m_incl, scores[:CHUNK], 0.0), jnp.where(m_incl, scores[CHUNK:], 0.0)]
            v_pair = jnp.concatenate([u["v"][:, vs][:, :GLA_DV], u["v"][:, vs][:, GLA_DV:]], axis=0)
            kv.append(_dot_tn(jnp.concatenate(k_pair, axis=0), v_pair))
        u["q_sel"] = q_sel
        u["att"] = att
        u["kv"] = jnp.concatenate(kv, axis=0)

    pairs = []
    zero = jnp.zeros((CHUNK, GDN_DK), F32)
    zero2 = jnp.zeros((CHUNK, 2 * GDN_DV), F32)
    for u in units:
        d = u["d"]
        _, _, _, (m_incl2, m_strict2), last = consts[d]
        y = u["y"]
        for pr in range(GDN_HEADS // 2):
            hs = (2 * pr, 2 * pr + 1)
            q = [y[:, h * GDN_DK:(h + 1) * GDN_DK] for h in hs]
            k = [y[:, GDN_QK + h * GDN_DK:GDN_QK + (h + 1) * GDN_DK] for h in hs]
            v = [y[:, 2 * GDN_QK + h * GDN_DV:2 * GDN_QK + (h + 1) * GDN_DV] for h in hs]
            g_bc = [jnp.broadcast_to(u["g_cols"][:, SM_GDEC + GDN_HEADS * d + h:SM_GDEC + GDN_HEADS * d + h + 1],
                                     (CHUNK, GDN_DK)) for h in hs]
            b_bc = [jnp.broadcast_to(u["small"][:, SM_BETA + GDN_HEADS * d + h:SM_BETA + GDN_HEADS * d + h + 1],
                                     (CHUNK, GDN_DK)) for h in hs]
            g_row = [u["g_rows"][GDN_HEADS * d + h:GDN_HEADS * d + h + 1, :] for h in hs]
            g2 = jnp.where(lane_lo, g_bc[0], g_bc[1])
            b2 = jnp.where(lane_lo, b_bc[0], b_bc[1])
            decay2 = jnp.where(m_incl2, jnp.exp(g2 - jnp.concatenate(g_row, axis=1)), 0.0)
            e_g = [jnp.exp(g) for g in g_bc]
            g_last = [g[:, last:last + 1] for g in g_row]
            lhs = jnp.concatenate([jnp.concatenate([q[0], k[0]], axis=0),
                                   jnp.concatenate([q[1], k[1]], axis=0)], axis=1)
            k_bd = jnp.concatenate([jnp.concatenate([k[0], zero], axis=1),
                                    jnp.concatenate([zero, k[1]], axis=1)], axis=0)
            p2 = _dot_nt(lhs, k_bd)
            rhs = [jnp.concatenate([v[i] * b_bc[i], k[i] * (b_bc[i] * e_g[i])], axis=1) for i in range(2)]
            pairs.append(dict(
                u=u, hs=hs, qk=p2[:CHUNK] * decay2,
                a=jnp.where(m_strict2, p2[CHUNK:] * b2 * decay2, 0.0),
                rhs=jnp.concatenate([jnp.concatenate([rhs[0], zero2], axis=1),
                                     jnp.concatenate([zero2, rhs[1]], axis=1)], axis=0),
                q_dec=[q[i] * e_g[i] for i in range(2)],
                k_dec=[k[i] * jnp.exp(g_last[i] - g_bc[i]) for i in range(2)],
                e_last=[jnp.exp(g) for g in g_last]))
    t_list = _inv_unit_triangular([c["a"] for c in pairs], bd_mask2, eye2, lane_lo)
    for c, t in zip(pairs, t_list):
        c["uw"] = _dot(t, c["rhs"])

    sg = [s_gla[0], s_gla[1]]
    sd = [[s_gdn[d, h] for h in range(GDN_HEADS)] for d in range(2)]
    per_step = 2 * (GDN_HEADS // 2)
    width = 2 * GDN_DV
    for step in range(n_chunks):
        for d in range(2):
            u = units[2 * step + d]
            o_ref = refs[d][5]
            s = sg[d]
            for h in range(GLA_HEADS):
                ps = slice((h // 2) * 2 * GLA_DK, (h // 2 + 1) * 2 * GLA_DK)
                vs = slice(h * GLA_DV, (h + 1) * GLA_DV)
                lhs = jnp.concatenate([u["q_sel"][h], u["att"][h]], axis=1)
                rhs = jnp.concatenate([s[ps], u["v"][:, vs]], axis=0)
                o_ref[0, u["rows"], vs] = _dot(lhs, rhs)
            sg[d] = jnp.exp(u["total"]) * s + u["kv"]
        step_pairs = pairs[per_step * step:per_step * (step + 1)]
        cs = [(c, i) for c in step_pairs for i in range(2)]
        r = [_dot(jnp.concatenate([c["uw"][:, i * width + GDN_DV:(i + 1) * width], c["q_dec"][i]], axis=0),
                  sd[c["u"]["d"]][c["hs"][i]]) for c, i in cs]
        v_new = [c["uw"][:, i * width:i * width + GDN_DV] - ri_[:CHUNK] for (c, i), ri_ in zip(cs, r)]
        for n_, c in enumerate(step_pairs):
            d = c["u"]["d"]
            o_ref = refs[d][5]
            qk_sel = jnp.concatenate([jnp.where(lane_lo, c["qk"], 0.0), jnp.where(lane_lo, 0.0, c["qk"])], axis=0)
            intra = _dot(qk_sel, jnp.concatenate(v_new[2 * n_:2 * n_ + 2], axis=0))
            for i in range(2):
                h = c["hs"][i]
                o_ref[0, c["u"]["rows"], GLA_V + h * GDN_DV:GLA_V + (h + 1) * GDN_DV] = (
                    r[2 * n_ + i][CHUNK:] + intra[i * CHUNK:(i + 1) * CHUNK])
                sd[d][h] = c["e_last"][i] * sd[d][h] + _dot_tn(c["k_dec"][i], v_new[2 * n_ + i])
    for d in range(2):
        s_gla[d] = sg[d]
        for h in range(GDN_HEADS):
            s_gdn[d, h] = sd[d][h]


def _scan(qkv, lff, lfb, gdn, small, small_t, tb):
    bsz, seq, _ = qkv.shape
    nb = seq // tb
    fwd = lambda w: pl.BlockSpec((1, tb, w), lambda b, j: (b, j, 0))
    bwd = lambda w: pl.BlockSpec((1, tb, w), lambda b, j: (b, nb - 1 - j, 0))
    rows = 4 * GDN_HEADS
    in_specs = [fwd(qkv.shape[2]), fwd(GLA_QK), fwd(GDN_CONV_W), fwd(SM_W),
                pl.BlockSpec((1, rows, tb), lambda b, j: (b, 0, j)),
                bwd(qkv.shape[2]), bwd(GLA_QK), bwd(GDN_CONV_W), bwd(SM_W),
                pl.BlockSpec((1, rows, tb), lambda b, j: (b, 0, nb - 1 - j))]
    width = GLA_V + GDN_V
    return pl.pallas_call(
        _scan_kernel,
        grid=(bsz, nb),
        in_specs=in_specs,
        out_specs=(fwd(width), bwd(width)),
        out_shape=(jax.ShapeDtypeStruct((bsz, seq, width), F32),) * 2,
        scratch_shapes=[pltpu.VMEM((2, GLA_QK, GLA_DV), F32),
                        pltpu.VMEM((2, GDN_HEADS, GDN_DK, GDN_DV), F32)],
        compiler_params=pltpu.CompilerParams(
            dimension_semantics=("arbitrary", "arbitrary"), vmem_limit_bytes=VMEM_LIMIT),
        name="scan",
    )(qkv, lff, gdn, small, small_t, qkv, lfb, gdn, small, small_t)


def _mix_ffn_kernel(of_ref, ofp_ref, ofn_ref, ob_ref, obp_ref, obn_ref, g_ref, gp_ref, gx_ref,
                    x_ref, xp_ref, xn_ref, gn_ref, wout_ref, nw_ref, wup_ref, conv_ref, wdown_ref,
                    fw_ref, out_ref):
    j = pl.program_id(1)
    nj = pl.num_programs(1)
    tm = x_ref.shape[1]
    half = tm // 2
    ext = half + 2 * HALO
    n_tiles = D_FF // FF_TILE
    gn = gn_ref[...]
    row = lax.broadcasted_iota(jnp.int32, (ext, D_MODEL), 0)
    pad_row = ((row == HALO - 1) & (j == 0), (row == HALO + half) & (j == nj - 1))

    def frame(idx, main_ref, prev, nxt):
        if idx == 0:
            return jnp.concatenate([prev, main_ref[0, :half + HALO, :]], axis=0)
        return jnp.concatenate([main_ref[0, half - HALO:, :], nxt], axis=0)

    def gate_frame(idx):
        if idx == 0:
            main = g_ref[0, :half + 2 * HALO, :].astype(F32)[:half + HALO]
            return jnp.concatenate([gp_ref[0].astype(F32)[HALO:], main], axis=0)
        main = g_ref[0, half - 2 * HALO:, :].astype(F32)[HALO:]
        return jnp.concatenate([main, gx_ref[0].astype(F32)[:HALO]], axis=0)

    state = {}

    def prologue(idx):
        o = (frame(idx, of_ref, ofp_ref[0], ofn_ref[0])
             + frame(idx, ob_ref, obp_ref[0], obn_ref[0]))
        g = gate_frame(idx)
        parts = []
        for h in range(GLA_HEADS + GDN_HEADS):
            hs = slice(h * GLA_DV, (h + 1) * GLA_DV)
            gate = g[:, hs]
            parts.append(_rms(o[:, hs], gn[:, hs]) * (gate * _sigmoid(gate)))
        y = jnp.concatenate(parts, axis=-1).astype(BF16)
        x1 = (frame(idx, x_ref, xp_ref[0], xn_ref[0])
              + jnp.dot(y, wout_ref[...], preferred_element_type=F32))
        he = jnp.where(pad_row[idx], 0.0, _rms(x1, nw_ref[...])).astype(BF16)
        state[idx] = (x1[HALO:HALO + half], he)

    def ffn_half(idx, hook):
        x1, he = state[idx]

        def up(t):
            cols = [slice(part * D_FF + t * FF_TILE, part * D_FF + (t + 1) * FF_TILE) for part in range(2)]
            ue = [jnp.dot(he, wup_ref[:, c], preferred_element_type=F32) for c in cols]
            return ue, [conv_ref[:, c] for c in cols]

        queue = [up(0), up(1)]
        acc = None
        for t in range(n_tiles):
            if t + 2 < n_tiles:
                queue.append(up(t + 2))
            ue, cw = queue.pop(0)
            if t == n_tiles // 2 and hook is not None:
                hook()
            val, gate = [(pltpu.roll(e, 1, 0) * w[0:1] + e * w[1:2] + pltpu.roll(e, ext - 1, 0) * w[2:3])
                         [HALO:HALO + half] for e, w in zip(ue, cw)]
            act = (gate * _sigmoid(gate) * val).astype(BF16)
            part = jnp.dot(act, wdown_ref[t * FF_TILE:(t + 1) * FF_TILE, :], preferred_element_type=F32)
            acc = part if acc is None else acc + part
        out_ref[0, idx * half:(idx + 1) * half, :] = _rms(x1 + acc, fw_ref[...])

    prologue(0)
    ffn_half(0, lambda: prologue(1))
    ffn_half(1, None)


def _mix_ffn(o_f, o_b, gates, x, gn, wout, nw, wup, conv, wdown, fw, tm):
    bsz, seq, _ = x.shape
    prev, nxt = _halo_specs(tm, seq, D_MODEL, HALO)
    prev16, nxt16 = _halo_specs(tm, seq, D_MODEL, 2 * HALO)
    tok = pl.BlockSpec((1, tm, D_MODEL), lambda b, j: (b, j, 0))
    once = lambda a: pl.BlockSpec(a.shape, lambda b, j: (0,) * a.ndim, pipeline_mode=pl.Buffered(1))
    in_specs = ([tok, prev, nxt, tok, prev, nxt, tok, prev16, nxt16, tok, prev, nxt]
                + [_const_spec(gn.shape), once(wout), _const_spec(nw.shape), once(wup),
                   _const_spec(conv.shape), once(wdown), _const_spec(fw.shape)])
    return pl.pallas_call(
        _mix_ffn_kernel,
        grid=(bsz, seq // tm),
        in_specs=in_specs,
        out_specs=tok,
        out_shape=jax.ShapeDtypeStruct((bsz, seq, D_MODEL), F32),
        compiler_params=pltpu.CompilerParams(
            dimension_semantics=("arbitrary", "arbitrary"), vmem_limit_bytes=VMEM_LIMIT),
        name="mix_ffn",
    )(o_f, o_f, o_f, o_b, o_b, o_b, gates, gates, gates, x, x, x, gn, wout, nw, wup, conv, wdown, fw)


def _stage_params(attn_norm, w_in, gla_gate_w, gla_gate_b, gla_norm, gdn_conv, gdn_a_log, gdn_dt_bias,
                  gdn_norm, w_out, ffn_norm, w_up, ffn_conv, w_down, final_norm):
    w = w_in[0]
    o_gg = 2 * GLA_QK + GLA_V
    o_lr = o_gg + GLA_V
    o_dq = o_lr + 2 * GLA_RANK
    o_dg = o_dq + GDN_CONV_W
    o_da = o_dg + GDN_V
    wa = w[:, :o_gg].astype(BF16)
    wg = jnp.concatenate([w[:, o_gg:o_lr], w[:, o_dg:o_da]], axis=1).astype(BF16)
    wc = w[:, o_dq:o_dg].astype(BF16)
    ws = jnp.concatenate([w[:, o_lr:o_dq], w[:, o_da:],
                          jnp.zeros((D_MODEL, SM_W - 2 * GLA_RANK - 4 * GDN_HEADS), F32)], axis=1).astype(BF16)
    gw = jnp.zeros((SM_W, 2 * GLA_QK), F32)
    gw = gw.at[:GLA_RANK, :GLA_QK].set(gla_gate_w[0, 0])
    gw = gw.at[GLA_RANK:2 * GLA_RANK, GLA_QK:].set(gla_gate_w[0, 1]).astype(BF16)
    gb = gla_gate_b[0].reshape(1, 2 * GLA_QK)
    pad8 = lambda a: jnp.concatenate([a, jnp.zeros((HALO - a.shape[0],) + a.shape[1:], a.dtype)], axis=0)
    conv = pad8(gdn_conv[0])
    lanes = jnp.zeros((1, SM_W), F32)
    alog = lanes.at[0, SM_GDEC:SM_BETA].set(gdn_a_log[0].reshape(-1))
    dtb = lanes.at[0, SM_GDEC:SM_BETA].set(gdn_dt_bias[0].reshape(-1))
    gn = jnp.concatenate([jnp.tile(gla_norm[0], GLA_HEADS), jnp.tile(gdn_norm[0], GDN_HEADS)]).reshape(1, -1)
    return dict(
        attn_norm=attn_norm[0].reshape(1, -1), wa=wa, wg=wg, wc=wc, ws=ws, gw=gw, gb=gb, conv=conv,
        alog=alog, dtb=dtb, gn=gn, wout=w_out[0].astype(BF16), ffn_norm=ffn_norm[0].reshape(1, -1),
        wup=w_up[0].astype(BF16), fconv=pad8(ffn_conv[0]), wdown=w_down[0].astype(BF16),
        final_norm=final_norm.reshape(1, -1))


def _tile(seq, want):
    t = min(want, seq)
    assert seq % t == 0 and t % 16 == 0
    return t


def _encode(x, p):
    seq = x.shape[1]
    tm = _tile(seq, 512)
    qkv, lff, lfb, gdn, gates, small, small_t = _inproj(
        x, p["attn_norm"], p["wa"], p["wg"], p["wc"], p["ws"], p["gw"], p["gb"], p["conv"],
        p["alog"], p["dtb"], tm)
    o_f, o_b = _scan(qkv, lff, lfb, gdn, small, small_t, tm)
    return _mix_ffn(o_f, o_b, gates, x, p["gn"], p["wout"], p["ffn_norm"], p["wup"], p["fconv"], p["wdown"],
                    p["final_norm"], tm)


def kernel(x_prompt, x_sample, attn_norm, w_in, gla_gate_w, gla_gate_b, gla_norm, gdn_conv, gdn_a_log,
           gdn_dt_bias, gdn_norm, w_out, ffn_norm, w_up, ffn_conv, w_down, final_norm):
    p = _stage_params(attn_norm, w_in, gla_gate_w, gla_gate_b, gla_norm, gdn_conv, gdn_a_log, gdn_dt_bias,
                      gdn_norm, w_out, ffn_norm, w_up, ffn_conv, w_down, final_norm)
    return (_encode(x_prompt, p), _encode(x_sample, p))
```

```python
import jax
import jax.numpy as jnp
from jax import lax
from jax.experimental import pallas as pl
from jax.experimental.pallas import tpu as pltpu

D_MODEL = 1024
CHUNK = 64
GLA_HEADS = 4
GLA_DK = 64
GLA_DV = 128
GLA_RANK = 16
GLA_GATE_NORM = 16.0
GDN_HEADS = 4
GDN_DK = 128
GDN_DV = 128
D_FF = 2816
EPS = 1e-6

GLA_QK = GLA_HEADS * GLA_DK
GLA_V = GLA_HEADS * GLA_DV
GDN_QK = GDN_HEADS * GDN_DK
GDN_V = GDN_HEADS * GDN_DV
GDN_CONV_W = 2 * GDN_QK + GDN_V

SM_W = 128
SM_GDEC = 2 * GLA_RANK
SM_BETA = SM_GDEC + 2 * GDN_HEADS

HALO = 8
FF_TILE = 256
SUB = 16

VMEM_LIMIT = 56 * 1024 * 1024

F32 = jnp.float32
BF16 = jnp.bfloat16


def _dot(a, b):
    return jnp.dot(a.astype(BF16), b.astype(BF16), preferred_element_type=F32)


def _dot_nt(a, b):
    return lax.dot_general(a.astype(BF16), b.astype(BF16), (((1,), (1,)), ((), ())),
                           preferred_element_type=F32)


def _dot_tn(a, b):
    return lax.dot_general(a.astype(BF16), b.astype(BF16), (((0,), (0,)), ((), ())),
                           preferred_element_type=F32)


def _split(a):
    hi = a.astype(BF16)
    lo = (a - hi.astype(F32)).astype(BF16)
    return hi, lo


def _sigmoid(x):
    return 1.0 / (1.0 + jnp.exp(-x))


def _softplus(x):
    return jnp.maximum(x, 0.0) + jnp.log1p(jnp.exp(-jnp.abs(x)))


def _rms(x, w):
    return x * lax.rsqrt(jnp.mean(x * x, axis=-1, keepdims=True) + EPS) * w


def _inproj_kernel(x_ref, xp_ref, xn_ref, nw_ref, wa_ref, wg_ref, wc_ref, ws_ref, gw_ref, gb_ref,
                   conv_ref, alog_ref, dtb_ref,
                   qkv_ref, lff_ref, lfb_ref, gdn_ref, gates_ref, small_ref, smallt_ref):
    j = pl.program_id(1)
    nj = pl.num_programs(1)
    tm = x_ref.shape[1]

    xe = jnp.concatenate([x_ref[0], xp_ref[0], xn_ref[0]], axis=0)
    he = _rms(xe, nw_ref[...]).astype(BF16)
    h = he[:tm]

    row = lax.broadcasted_iota(jnp.int32, (tm, GDN_QK), 0)
    first_row = row == 0
    last_row = row == tm - 1

    def conv_group(g):
        cols = slice(g * GDN_QK, (g + 1) * GDN_QK)
        ce = jnp.dot(he, wc_ref[:, cols], preferred_element_type=F32)
        c = ce[:tm]
        c_before = jnp.where(j == 0, 0.0, ce[tm + HALO - 1:tm + HALO])
        c_after = jnp.where(j == nj - 1, 0.0, ce[tm + HALO:tm + HALO + 1])
        c_prev = jnp.where(first_row, c_before, pltpu.roll(c, 1, 0))
        c_next = jnp.where(last_row, c_after, pltpu.roll(c, tm - 1, 0))
        cw = conv_ref[:, cols]
        y = c_prev * cw[0:1] + c * cw[1:2] + c_next * cw[2:3]
        return y * _sigmoid(y)

    def store_normalised(g, y, scale):
        for i in range(GDN_HEADS):
            seg = y[:, i * GDN_DK:(i + 1) * GDN_DK]
            seg = seg * (lax.rsqrt(jnp.sum(seg * seg, axis=-1, keepdims=True) + EPS) * scale)
            gdn_ref[0, :, g * GDN_QK + i * GDN_DK:g * GDN_QK + (i + 1) * GDN_DK] = seg.astype(BF16)

    a = jnp.dot(h, wa_ref[...], preferred_element_type=F32)
    qkv_ref[0, :, :GLA_QK] = (a[:, :GLA_QK] * (GLA_DK ** -0.5)).astype(BF16)
    qkv_ref[0, :, GLA_QK:] = a[:, GLA_QK:].astype(BF16)

    gates_ref[0] = jnp.dot(h, wg_ref[...], preferred_element_type=F32).astype(BF16)

    s = jnp.dot(h, ws_ref[...], preferred_element_type=F32)
    logits = jnp.dot(s.astype(BF16), gw_ref[...], preferred_element_type=F32) + gb_ref[...]
    store_normalised(0, conv_group(0), GDN_DK ** -0.5)
    store_normalised(1, conv_group(1), 1.0)
    gdn_ref[0, :, 2 * GDN_QK:] = conv_group(2).astype(BF16)

    log_f = (jnp.minimum(logits, 0.0) - jnp.log1p(jnp.exp(-jnp.abs(logits)))) * (1.0 / GLA_GATE_NORM)
    lff_ref[0] = log_f[:, :GLA_QK]
    lfb_ref[0] = log_f[:, GLA_QK:]

    lane = lax.broadcasted_iota(jnp.int32, s.shape, 1)
    gdec = -jnp.exp(alog_ref[...]) * _softplus(s + dtb_ref[...])
    beta = _sigmoid(s)
    small = jnp.where((lane >= SM_GDEC) & (lane < SM_BETA), gdec,
                      jnp.where((lane >= SM_BETA) & (lane < SM_BETA + 2 * GDN_HEADS), beta, 0.0))
    small_ref[0] = small
    smallt_ref[0] = jnp.transpose(small)[SM_GDEC:SM_GDEC + 4 * GDN_HEADS]


def _halo_specs(tm, seq, width, rows):
    per = tm // rows
    last = seq // rows - 1
    prev = pl.BlockSpec((1, rows, width), lambda b, j: (b, jnp.maximum(j * per - 1, 0), 0))
    nxt = pl.BlockSpec((1, rows, width), lambda b, j: (b, jnp.minimum((j + 1) * per, last), 0))
    return prev, nxt


def _const_spec(shape):
    return pl.BlockSpec(shape, lambda b, j: (0,) * len(shape))


def _inproj(x, nw, wa, wg, wc, ws, gw, gb, conv, alog, dtb, tm):
    bsz, seq, _ = x.shape
    prev, nxt = _halo_specs(tm, seq, D_MODEL, HALO)
    tok = lambda w: pl.BlockSpec((1, tm, w), lambda b, j: (b, j, 0))
    out_shape = (
        jax.ShapeDtypeStruct((bsz, seq, GLA_QK * 2 + GLA_V), BF16),
        jax.ShapeDtypeStruct((bsz, seq, GLA_QK), F32),
        jax.ShapeDtypeStruct((bsz, seq, GLA_QK), F32),
        jax.ShapeDtypeStruct((bsz, seq, GDN_CONV_W), BF16),
        jax.ShapeDtypeStruct((bsz, seq, GLA_V + GDN_V), BF16),
        jax.ShapeDtypeStruct((bsz, seq, SM_W), F32),
        jax.ShapeDtypeStruct((bsz, 4 * GDN_HEADS, seq), F32),
    )
    out_specs = (tok(GLA_QK * 2 + GLA_V), tok(GLA_QK), tok(GLA_QK), tok(GDN_CONV_W),
                 tok(GLA_V + GDN_V), tok(SM_W),
                 pl.BlockSpec((1, 4 * GDN_HEADS, tm), lambda b, j: (b, 0, j)))
    in_specs = [tok(D_MODEL), prev, nxt] + [_const_spec(a.shape) for a in
                                            (nw, wa, wg, wc, ws, gw, gb, conv, alog, dtb)]
    return pl.pallas_call(
        _inproj_kernel,
        grid=(bsz, seq // tm),
        in_specs=in_specs,
        out_specs=out_specs,
        out_shape=out_shape,
        compiler_params=pltpu.CompilerParams(
            dimension_semantics=("arbitrary", "arbitrary"), vmem_limit_bytes=VMEM_LIMIT),
        name="inproj",
    )(x, x, x, nw, wa, wg, wc, ws, gw, gb, conv, alog, dtb)


def _mm(a, b):
    return jnp.dot(a, b, preferred_element_type=F32)


def _inv_unit_triangular(a_list, bd_mask, eye, lane_lo):
    def bd(y):
        return jnp.concatenate([jnp.where(lane_lo, y, 0.0).astype(BF16),
                                jnp.where(lane_lo, 0.0, y).astype(BF16)], axis=0)

    a_bd = [jnp.where(bd_mask, a, 0.0) for a in a_list]
    a_off = [bd(a - b) for a, b in zip(a_list, a_bd)]
    d = [eye - b for b in a_bd]
    x1 = [-b for b in a_bd]
    x2 = [_mm(v.astype(BF16), bd(v)) for v in x1]
    both = [_mm(jnp.concatenate([v, di], axis=0).astype(BF16), bd(v)) for v, di in zip(x2, d)]
    x4 = [b[:CHUNK] for b in both]
    d = [di + b[CHUNK:] for di, b in zip(d, both)]
    both = [_mm(jnp.concatenate([v, di], axis=0).astype(BF16), bd(v)) for v, di in zip(x4, d)]
    x8 = [b[:CHUNK] for b in both]
    d = [di + b[CHUNK:] for di, b in zip(d, both)]
    d = [di + _mm(di.astype(BF16), bd(v)) for di, v in zip(d, x8)]
    n = [_mm(di.astype(BF16), ao) for di, ao in zip(d, a_off)]
    n2 = [_mm(v.astype(BF16), bd(v)) for v in n]
    q = [eye - v for v in n]
    q = [qi + _mm(qi.astype(BF16), bd(v)) for qi, v in zip(q, n2)]
    return [_mm(qi.astype(BF16), bd(di)) for qi, di in zip(q, d)]


def _scan_kernel(qkv_f, lf_f, gdn_f, sm_f, smt_f, qkv_b, lf_b, gdn_b, sm_b, smt_b,
                 o_f, o_b, s_gla, s_gdn):
    @pl.when(pl.program_id(1) == 0)
    def _():
        s_gla[...] = jnp.zeros_like(s_gla)
        s_gdn[...] = jnp.zeros_like(s_gdn)

    n_chunks = qkv_f.shape[1] // CHUNK
    ri = lax.broadcasted_iota(jnp.int32, (CHUNK, CHUNK), 0)
    ci = lax.broadcasted_iota(jnp.int32, (CHUNK, CHUNK), 1)
    lower = ri >= ci
    upper = ri <= ci
    ri2 = lax.broadcasted_iota(jnp.int32, (CHUNK, 2 * CHUNK), 0)
    lane2 = lax.broadcasted_iota(jnp.int32, (CHUNK, 2 * CHUNK), 1)
    lane_lo = lane2 < CHUNK
    ci2 = jnp.where(lane_lo, lane2, lane2 - CHUNK)
    eye2 = jnp.where(ri2 == ci2, 1.0, 0.0).astype(F32)
    bd_mask2 = (ri2 // SUB) == (ci2 // SUB)
    tri_lo = jnp.where(lower, 1.0, 0.0).astype(BF16)
    tri_up = jnp.where(upper, 1.0, 0.0).astype(BF16)
    ones_bc = jnp.ones((CHUNK, GLA_DV), BF16)
    tn = (((0,), (0,)), ((), ()))

    refs = ((qkv_f, lf_f, gdn_f, sm_f, smt_f, o_f), (qkv_b, lf_b, gdn_b, sm_b, smt_b, o_b))
    consts = ((tri_lo, tri_up, lower, (ri2 >= ci2, ri2 > ci2), CHUNK - 1),
              (tri_up, tri_lo, upper, (ri2 <= ci2, ri2 < ci2), 0))

    units = []
    for step in range(n_chunks):
        for d in range(2):
            qkv, lf, gdn, sm, smt, _ = refs[d]
            tri, tri_t, _, _, _ = consts[d]
            c = step if d == 0 else n_chunks - 1 - step
            rows = slice(c * CHUNK, (c + 1) * CHUNK)
            x = qkv[0, rows, :].astype(F32)
            lf_hi, lf_lo = _split(lf[0, rows, :])
            small = sm[0, rows, :]
            sm_hi, sm_lo = _split(small)
            smt_hi, smt_lo = _split(smt[0, :, rows])
            units.append(dict(
                d=d, rows=rows, q=x[:, :GLA_QK], k=x[:, GLA_QK:2 * GLA_QK], v=x[:, 2 * GLA_QK:],
                y=gdn[0, rows, :].astype(F32), small=small,
                cum=_mm(tri, lf_hi) + _mm(tri, lf_lo),
                total=(lax.dot_general(lf_hi, ones_bc, tn, preferred_element_type=F32)
                       + lax.dot_general(lf_lo, ones_bc, tn, preferred_element_type=F32)),
                g_cols=_mm(tri, sm_hi) + _mm(tri, sm_lo),
                g_rows=_mm(smt_hi, tri_t) + _mm(smt_lo, tri_t)))

    first_of_pair = lax.broadcasted_iota(jnp.int32, (CHUNK, 2 * GLA_DK), 1) < GLA_DK
    half_row = lax.broadcasted_iota(jnp.int32, (CHUNK, GLA_QK), 0)
    for u in units:
        _, _, m_incl, _, last = consts[u["d"]]
        cum = u["cum"]
        q_in = u["q"] * jnp.exp(cum)
        k_out = u["k"] * jnp.exp(cum[last:last + 1] - cum)
        if u["d"] == 0:
            mid = cum[CHUNK // 2 - 1:CHUNK // 2]
            near = half_row < CHUNK // 2
        else:
            mid = cum[CHUNK // 2:CHUNK // 2 + 1]
            near = half_row >= CHUNK // 2
        q_att = [jnp.where(near, q_in, 0.0), jnp.where(near, 0.0, u["q"] * jnp.exp(cum - mid))]
        k_att = [jnp.where(near, u["k"] * jnp.exp(-cum), 0.0), u["k"] * jnp.exp(mid - cum)]
        q_sel, att, kv = [], [], []
        for pair in range(GLA_HEADS // 2):
            ps = slice(pair * 2 * GLA_DK, (pair + 1) * 2 * GLA_DK)
            vs = slice(pair * 2 * GLA_DV, (pair + 1) * 2 * GLA_DV)
            q_pair = [jnp.where(first_of_pair, q_in[:, ps], 0.0), jnp.where(first_of_pair, 0.0, q_in[:, ps])]
            k_pair = [jnp.where(first_of_pair, k_out[:, ps], 0.0), jnp.where(first_of_pair, 0.0, k_out[:, ps])]
            qa = [jnp.concatenate([jnp.where(first_of_pair, qr[:, ps], 0.0) for qr in q_att], axis=1),
                  jnp.concatenate([jnp.where(first_of_pair, 0.0, qr[:, ps]) for qr in q_att], axis=1)]
            scores = _dot_nt(jnp.concatenate(qa, axis=0),
                             jnp.concatenate([kr[:, ps] for kr in k_att], axis=1))
            q_sel += q_pair
            att += [jnp.where(m_incl, scores[:CHUNK], 0.0), jnp.where(m_incl, scores[CHUNK:], 0.0)]
            v_pair = jnp.concatenate([u["v"][:, vs][:, :GLA_DV], u["v"][:, vs][:, GLA_DV:]], axis=0)
            kv.append(_dot_tn(jnp.concatenate(k_pair, axis=0), v_pair))
        u["q_sel"] = q_sel
        u["att"] = att
        u["kv"] = jnp.concatenate(kv, axis=0)

    pairs = []
    zero = jnp.zeros((CHUNK, GDN_DK), F32)
    zero2 = jnp.zeros((CHUNK, 2 * GDN_DV), F32)
    for u in units:
        d = u["d"]
        _, _, _, (m_incl2, m_strict2), last = consts[d]
        y = u["y"]
        for pr in range(GDN_HEADS // 2):
            hs = (2 * pr, 2 * pr + 1)
            q = [y[:, h * GDN_DK:(h + 1) * GDN_DK] for h in hs]
            k = [y[:, GDN_QK + h * GDN_DK:GDN_QK + (h + 1) * GDN_DK] for h in hs]
            v = [y[:, 2 * GDN_QK + h * GDN_DV:2 * GDN_QK + (h + 1) * GDN_DV] for h in hs]
            g_bc = [jnp.broadcast_to(u["g_cols"][:, SM_GDEC + GDN_HEADS * d + h:SM_GDEC + GDN_HEADS * d + h + 1],
                                     (CHUNK, GDN_DK)) for h in hs]
            b_bc = [jnp.broadcast_to(u["small"][:, SM_BETA + GDN_HEADS * d + h:SM_BETA + GDN_HEADS * d + h + 1],
                                     (CHUNK, GDN_DK)) for h in hs]
            g_row = [u["g_rows"][GDN_HEADS * d + h:GDN_HEADS * d + h + 1, :] for h in hs]
            g2 = jnp.where(lane_lo, g_bc[0], g_bc[1])
            b2 = jnp.where(lane_lo, b_bc[0], b_bc[1])
            decay2 = jnp.where(m_incl2, jnp.exp(g2 - jnp.concatenate(g_row, axis=1)), 0.0)
            e_g = [jnp.exp(g) for g in g_bc]
            g_last = [g[:, last:last + 1] for g in g_row]
            lhs = jnp.concatenate([jnp.concatenate([q[0], k[0]], axis=0),
                                   jnp.concatenate([q[1], k[1]], axis=0)], axis=1)
            k_bd = jnp.concatenate([jnp.concatenate([k[0], zero], axis=1),
                                    jnp.concatenate([zero, k[1]], axis=1)], axis=0)
            p2 = _dot_nt(lhs, k_bd)
            rhs = [jnp.concatenate([v[i] * b_bc[i], k[i] * (b_bc[i] * e_g[i])], axis=1) for i in range(2)]
            pairs.append(dict(
                u=u, hs=hs, qk=p2[:CHUNK] * decay2,
                a=jnp.where(m_strict2, p2[CHUNK:] * b2 * decay2, 0.0),
                rhs=jnp.concatenate([jnp.concatenate([rhs[0], zero2], axis=1),
                                     jnp.concatenate([zero2, rhs[1]], axis=1)], axis=0),
                q_dec=[q[i] * e_g[i] for i in range(2)],
                k_dec=[k[i] * jnp.exp(g_last[i] - g_bc[i]) for i in range(2)],
                e_last=[jnp.exp(g) for g in g_last]))
    t_list = _inv_unit_triangular([c["a"] for c in pairs], bd_mask2, eye2, lane_lo)
    for c, t in zip(pairs, t_list):
        c["uw"] = _dot(t, c["rhs"])

    sg = [s_gla[0], s_gla[1]]
    sd = [[s_gdn[d, h] for h in range(GDN_HEADS)] for d in range(2)]
    per_step = 2 * (GDN_HEADS // 2)
    width = 2 * GDN_DV
    for step in range(n_chunks):
        for d in range(2):
            u = units[2 * step + d]
            o_ref = refs[d][5]
            s = sg[d]
            for h in range(GLA_HEADS):
                ps = slice((h // 2) * 2 * GLA_DK, (h // 2 + 1) * 2 * GLA_DK)
                vs = slice(h * GLA_DV, (h + 1) * GLA_DV)
                lhs = jnp.concatenate([u["q_sel"][h], u["att"][h]], axis=1)
                rhs = jnp.concatenate([s[ps], u["v"][:, vs]], axis=0)
                o_ref[0, u["rows"], vs] = _dot(lhs, rhs)
            sg[d] = jnp.exp(u["total"]) * s + u["kv"]
        step_pairs = pairs[per_step * step:per_step * (step + 1)]
        cs = [(c, i) for c in step_pairs for i in range(2)]
        r = [_dot(jnp.concatenate([c["uw"][:, i * width + GDN_DV:(i + 1) * width], c["q_dec"][i]], axis=0),
                  sd[c["u"]["d"]][c["hs"][i]]) for c, i in cs]
        v_new = [c["uw"][:, i * width:i * width + GDN_DV] - ri_[:CHUNK] for (c, i), ri_ in zip(cs, r)]
        for n_, c in enumerate(step_pairs):
            d = c["u"]["d"]
            o_ref = refs[d][5]
            qk_sel = jnp.concatenate([jnp.where(lane_lo, c["qk"], 0.0), jnp.where(lane_lo, 0.0, c["qk"])], axis=0)
            intra = _dot(qk_sel, jnp.concatenate(v_new[2 * n_:2 * n_ + 2], axis=0))
            for i in range(2):
                h = c["hs"][i]
                o_ref[0, c["u"]["rows"], GLA_V + h * GDN_DV:GLA_V + (h + 1) * GDN_DV] = (
                    r[2 * n_ + i][CHUNK:] + intra[i * CHUNK:(i + 1) * CHUNK])
                sd[d][h] = c["e_last"][i] * sd[d][h] + _dot_tn(c["k_dec"][i], v_new[2 * n_ + i])
    for d in range(2):
        s_gla[d] = sg[d]
        for h in range(GDN_HEADS):
            s_gdn[d, h] = sd[d][h]


def _scan(qkv, lff, lfb, gdn, small, small_t, tb):
    bsz, seq, _ = qkv.shape
    nb = seq // tb
    fwd = lambda w: pl.BlockSpec((1, tb, w), lambda b, j: (b, j, 0))
    bwd = lambda w: pl.BlockSpec((1, tb, w), lambda b, j: (b, nb - 1 - j, 0))
    rows = 4 * GDN_HEADS
    in_specs = [fwd(qkv.shape[2]), fwd(GLA_QK), fwd(GDN_CONV_W), fwd(SM_W),
                pl.BlockSpec((1, rows, tb), lambda b, j: (b, 0, j)),
                bwd(qkv.shape[2]), bwd(GLA_QK), bwd(GDN_CONV_W), bwd(SM_W),
                pl.BlockSpec((1, rows, tb), lambda b, j: (b, 0, nb - 1 - j))]
    width = GLA_V + GDN_V
    return pl.pallas_call(
        _scan_kernel,
        grid=(bsz, nb),
        in_specs=in_specs,
        out_specs=(fwd(width), bwd(width)),
        out_shape=(jax.ShapeDtypeStruct((bsz, seq, width), F32),) * 2,
        scratch_shapes=[pltpu.VMEM((2, GLA_QK, GLA_DV), F32),
                        pltpu.VMEM((2, GDN_HEADS, GDN_DK, GDN_DV), F32)],
        compiler_params=pltpu.CompilerParams(
            dimension_semantics=("arbitrary", "arbitrary"), vmem_limit_bytes=VMEM_LIMIT),
        name="scan",
    )(qkv, lff, gdn, small, small_t, qkv, lfb, gdn, small, small_t)


def _mix_ffn_kernel(of_ref, ofp_ref, ofn_ref, ob_ref, obp_ref, obn_ref, g_ref, gp_ref, gx_ref,
                    x_ref, xp_ref, xn_ref, gn_ref, wout_ref, nw_ref, wup_ref, conv_ref, wdown_ref,
                    fw_ref, out_ref):
    j = pl.program_id(1)
    nj = pl.num_programs(1)
    tm = x_ref.shape[1]
    half = tm // 2
    ext = half + 2 * HALO
    n_tiles = D_FF // FF_TILE
    gn = gn_ref[...]
    row = lax.broadcasted_iota(jnp.int32, (ext, D_MODEL), 0)
    pad_row = ((row == HALO - 1) & (j == 0), (row == HALO + half) & (j == nj - 1))

    def frame(idx, main_ref, prev, nxt):
        if idx == 0:
            return jnp.concatenate([prev, main_ref[0, :half + HALO, :]], axis=0)
        return jnp.concatenate([main_ref[0, half - HALO:, :], nxt], axis=0)

    def gate_frame(idx):
        if idx == 0:
            main = g_ref[0, :half + 2 * HALO, :].astype(F32)[:half + HALO]
            return jnp.concatenate([gp_ref[0].astype(F32)[HALO:], main], axis=0)
        main = g_ref[0, half - 2 * HALO:, :].astype(F32)[HALO:]
        return jnp.concatenate([main, gx_ref[0].astype(F32)[:HALO]], axis=0)

    state = {}

    def prologue(idx):
        o = (frame(idx, of_ref, ofp_ref[0], ofn_ref[0])
             + frame(idx, ob_ref, obp_ref[0], obn_ref[0]))
        g = gate_frame(idx)
        parts = []
        for h in range(GLA_HEADS + GDN_HEADS):
            hs = slice(h * GLA_DV, (h + 1) * GLA_DV)
            gate = g[:, hs]
            parts.append(_rms(o[:, hs], gn[:, hs]) * (gate * _sigmoid(gate)))
        y = jnp.concatenate(parts, axis=-1).astype(BF16)
        x1 = (frame(idx, x_ref, xp_ref[0], xn_ref[0])
              + jnp.dot(y, wout_ref[...], preferred_element_type=F32))
        he = jnp.where(pad_row[idx], 0.0, _rms(x1, nw_ref[...])).astype(BF16)
        state[idx] = (x1[HALO:HALO + half], he)

    def ffn_half(idx, hook):
        x1, he = state[idx]

        def up(t):
            cols = [slice(part * D_FF + t * FF_TILE, part * D_FF + (t + 1) * FF_TILE) for part in range(2)]
            ue = [jnp.dot(he, wup_ref[:, c], preferred_element_type=F32) for c in cols]
            return ue, [conv_ref[:, c] for c in cols]

        queue = [up(0), up(1)]
        acc = None
        for t in range(n_tiles):
            if t + 2 < n_tiles:
                queue.append(up(t + 2))
            ue, cw = queue.pop(0)
            if t == n_tiles // 2 and hook is not None:
                hook()
            val, gate = [(pltpu.roll(e, 1, 0) * w[0:1] + e * w[1:2] + pltpu.roll(e, ext - 1, 0) * w[2:3])
                         [HALO:HALO + half] for e, w in zip(ue, cw)]
            act = (gate * _sigmoid(gate) * val).astype(BF16)
            part = jnp.dot(act, wdown_ref[t * FF_TILE:(t + 1) * FF_TILE, :], preferred_element_type=F32)
            acc = part if acc is None else acc + part
        out_ref[0, idx * half:(idx + 1) * half, :] = _rms(x1 + acc, fw_ref[...])

    prologue(0)
    ffn_half(0, lambda: prologue(1))
    ffn_half(1, None)


def _mix_ffn(o_f, o_b, gates, x, gn, wout, nw, wup, conv, wdown, fw, tm):
    bsz, seq, _ = x.shape
    prev, nxt = _halo_specs(tm, seq, D_MODEL, HALO)
    prev16, nxt16 = _halo_specs(tm, seq, D_MODEL, 2 * HALO)
    tok = pl.BlockSpec((1, tm, D_MODEL), lambda b, j: (b, j, 0))
    once = lambda a: pl.BlockSpec(a.shape, lambda b, j: (0,) * a.ndim, pipeline_mode=pl.Buffered(1))
    in_specs = ([tok, prev, nxt, tok, prev, nxt, tok, prev16, nxt16, tok, prev, nxt]
                + [_const_spec(gn.shape), once(wout), _const_spec(nw.shape), once(wup),
                   _const_spec(conv.shape), once(wdown), _const_spec(fw.shape)])
    return pl.pallas_call(
        _mix_ffn_kernel,
        grid=(bsz, seq // tm),
        in_specs=in_specs,
        out_specs=tok,
        out_shape=jax.ShapeDtypeStruct((bsz, seq, D_MODEL), F32),
        compiler_params=pltpu.CompilerParams(
            dimension_semantics=("arbitrary", "arbitrary"), vmem_limit_bytes=VMEM_LIMIT),
        name="mix_ffn",
    )(o_f, o_f, o_f, o_b, o_b, o_b, gates, gates, gates, x, x, x, gn, wout, nw, wup, conv, wdown, fw)


def _stage_params(attn_norm, w_in, gla_gate_w, gla_gate_b, gla_norm, gdn_conv, gdn_a_log, gdn_dt_bias,
                  gdn_norm, w_out, ffn_norm, w_up, ffn_conv, w_down, final_norm):
    w = w_in[0]
    o_gg = 2 * GLA_QK + GLA_V
    o_lr = o_gg + GLA_V
    o_dq = o_lr + 2 * GLA_RANK
    o_dg = o_dq + GDN_CONV_W
    o_da = o_dg + GDN_V
    wa = w[:, :o_gg].astype(BF16)
    wg = jnp.concatenate([w[:, o_gg:o_lr], w[:, o_dg:o_da]], axis=1).astype(BF16)
    wc = w[:, o_dq:o_dg].astype(BF16)
    ws = jnp.concatenate([w[:, o_lr:o_dq], w[:, o_da:],
                          jnp.zeros((D_MODEL, SM_W - 2 * GLA_RANK - 4 * GDN_HEADS), F32)], axis=1).astype(BF16)
    gw = jnp.zeros((SM_W, 2 * GLA_QK), F32)
    gw = gw.at[:GLA_RANK, :GLA_QK].set(gla_gate_w[0, 0])
    gw = gw.at[GLA_RANK:2 * GLA_RANK, GLA_QK:].set(gla_gate_w[0, 1]).astype(BF16)
    gb = gla_gate_b[0].reshape(1, 2 * GLA_QK)
    pad8 = lambda a: jnp.concatenate([a, jnp.zeros((HALO - a.shape[0],) + a.shape[1:], a.dtype)], axis=0)
    conv = pad8(gdn_conv[0])
    lanes = jnp.zeros((1, SM_W), F32)
    alog = lanes.at[0, SM_GDEC:SM_BETA].set(gdn_a_log[0].reshape(-1))
    dtb = lanes.at[0, SM_GDEC:SM_BETA].set(gdn_dt_bias[0].reshape(-1))
    gn = jnp.concatenate([jnp.tile(gla_norm[0], GLA_HEADS), jnp.tile(gdn_norm[0], GDN_HEADS)]).reshape(1, -1)
    return dict(
        attn_norm=attn_norm[0].reshape(1, -1), wa=wa, wg=wg, wc=wc, ws=ws, gw=gw, gb=gb, conv=conv,
        alog=alog, dtb=dtb, gn=gn, wout=w_out[0].astype(BF16), ffn_norm=ffn_norm[0].reshape(1, -1),
        wup=w_up[0].astype(BF16), fconv=pad8(ffn_conv[0]), wdown=w_down[0].astype(BF16),
        final_norm=final_norm.reshape(1, -1))


def _tile(seq, want):
    t = min(want, seq)
    assert seq % t == 0 and t % 16 == 0
    return t


def _encode(x, p):
    seq = x.shape[1]
    tm = _tile(seq, 512)
    qkv, lff, lfb, gdn, gates, small, small_t = _inproj(
        x, p["attn_norm"], p["wa"], p["wg"], p["wc"], p["ws"], p["gw"], p["gb"], p["conv"],
        p["alog"], p["dtb"], tm)
    o_f, o_b = _scan(qkv, lff, lfb, gdn, small, small_t, tm)
    return _mix_ffn(o_f, o_b, gates, x, p["gn"], p["wout"], p["ffn_norm"], p["wup"], p["fconv"], p["wdown"],
                    p["final_norm"], tm)


def kernel(x_prompt, x_sample, attn_norm, w_in, gla_gate_w, gla_gate_b, gla_norm, gdn_conv, gdn_a_log,
           gdn_dt_bias, gdn_norm, w_out, ffn_norm, w_up, ffn_conv, w_down, final_norm):
    p = _stage_params(attn_norm, w_in, gla_gate_w, gla_gate_b, gla_norm, gdn_conv, gdn_a_log, gdn_dt_bias,
                      gdn_norm, w_out, ffn_norm, w_up, ffn_conv, w_down, final_norm)
    return (_encode(x_prompt, p), _encode(x_sample, p))
```

```python
import jax
import jax.numpy as jnp
from jax import lax
from jax.experimental import pallas as pl
from jax.experimental.pallas import tpu as pltpu

D_MODEL = 1024
CHUNK = 64
GLA_HEADS = 4
GLA_DK = 64
GLA_DV = 128
GLA_RANK = 16
GLA_GATE_NORM = 16.0
GDN_HEADS = 4
GDN_DK = 128
GDN_DV = 128
D_FF = 2816
EPS = 1e-6

GLA_QK = GLA_HEADS * GLA_DK
GLA_V = GLA_HEADS * GLA_DV
GDN_QK = GDN_HEADS * GDN_DK
GDN_V = GDN_HEADS * GDN_DV
GDN_CONV_W = 2 * GDN_QK + GDN_V

SM_W = 128
SM_GDEC = 2 * GLA_RANK
SM_BETA = SM_GDEC + 2 * GDN_HEADS

HALO = 8
FF_TILE = 256
SUB = 16

VMEM_LIMIT = 56 * 1024 * 1024

F32 = jnp.float32
BF16 = jnp.bfloat16


def _dot(a, b):
    return jnp.dot(a.astype(BF16), b.astype(BF16), preferred_element_type=F32)


def _dot_nt(a, b):
    return lax.dot_general(a.astype(BF16), b.astype(BF16), (((1,), (1,)), ((), ())),
                           preferred_element_type=F32)


def _dot_tn(a, b):
    return lax.dot_general(a.astype(BF16), b.astype(BF16), (((0,), (0,)), ((), ())),
                           preferred_element_type=F32)


def _split(a):
    hi = a.astype(BF16)
    lo = (a - hi.astype(F32)).astype(BF16)
    return hi, lo


def _sigmoid(x):
    return 1.0 / (1.0 + jnp.exp(-x))


def _softplus(x):
    return jnp.maximum(x, 0.0) + jnp.log1p(jnp.exp(-jnp.abs(x)))


def _rms(x, w):
    return x * lax.rsqrt(jnp.mean(x * x, axis=-1, keepdims=True) + EPS) * w


def _inproj_kernel(x_ref, xp_ref, xn_ref, nw_ref, wa_ref, wg_ref, wc_ref, ws_ref, gw_ref, gb_ref,
                   conv_ref, alog_ref, dtb_ref,
                   qkv_ref, lff_ref, lfb_ref, gdn_ref, gates_ref, small_ref, smallt_ref):
    j = pl.program_id(1)
    nj = pl.num_programs(1)
    tm = x_ref.shape[1]

    xe = jnp.concatenate([x_ref[0], xp_ref[0], xn_ref[0]], axis=0)
    he = _rms(xe, nw_ref[...]).astype(BF16)
    h = he[:tm]

    row = lax.broadcasted_iota(jnp.int32, (tm, GDN_QK), 0)
    first_row = row == 0
    last_row = row == tm - 1

    def conv_group(g):
        cols = slice(g * GDN_QK, (g + 1) * GDN_QK)
        ce = jnp.dot(he, wc_ref[:, cols], preferred_element_type=F32)
        c = ce[:tm]
        c_before = jnp.where(j == 0, 0.0, ce[tm + HALO - 1:tm + HALO])
        c_after = jnp.where(j == nj - 1, 0.0, ce[tm + HALO:tm + HALO + 1])
        c_prev = jnp.where(first_row, c_before, pltpu.roll(c, 1, 0))
        c_next = jnp.where(last_row, c_after, pltpu.roll(c, tm - 1, 0))
        cw = conv_ref[:, cols]
        y = c_prev * cw[0:1] + c * cw[1:2] + c_next * cw[2:3]
        return y * _sigmoid(y)

    def store_normalised(g, y, scale):
        for i in range(GDN_HEADS):
            seg = y[:, i * GDN_DK:(i + 1) * GDN_DK]
            seg = seg * (lax.rsqrt(jnp.sum(seg * seg, axis=-1, keepdims=True) + EPS) * scale)
            gdn_ref[0, :, g * GDN_QK + i * GDN_DK:g * GDN_QK + (i + 1) * GDN_DK] = seg.astype(BF16)

    a = jnp.dot(h, wa_ref[...], preferred_element_type=F32)
    qkv_ref[0, :, :GLA_QK] = (a[:, :GLA_QK] * (GLA_DK ** -0.5)).astype(BF16)
    qkv_ref[0, :, GLA_QK:] = a[:, GLA_QK:].astype(BF16)

    gates_ref[0] = jnp.dot(h, wg_ref[...], preferred_element_type=F32).astype(BF16)

    s = jnp.dot(h, ws_ref[...], preferred_element_type=F32)
    logits = jnp.dot(s.astype(BF16), gw_ref[...], preferred_element_type=F32) + gb_ref[...]
    store_normalised(0, conv_group(0), GDN_DK ** -0.5)
    store_normalised(1, conv_group(1), 1.0)
    gdn_ref[0, :, 2 * GDN_QK:] = conv_group(2).astype(BF16)

    log_f = (jnp.minimum(logits, 0.0) - jnp.log1p(jnp.exp(-jnp.abs(logits)))) * (1.0 / GLA_GATE_NORM)
    lff_ref[0] = log_f[:, :GLA_QK]
    lfb_ref[0] = log_f[:, GLA_QK:]

    lane = lax.broadcasted_iota(jnp.int32, s.shape, 1)
    gdec = -jnp.exp(alog_ref[...]) * _softplus(s + dtb_ref[...])
    beta = _sigmoid(s)
    small = jnp.where((lane >= SM_GDEC) & (lane < SM_BETA), gdec,
                      jnp.where((lane >= SM_BETA) & (lane < SM_BETA + 2 * GDN_HEADS), beta, 0.0))
    small_ref[0] = small
    smallt_ref[0] = jnp.transpose(small)[SM_GDEC:SM_GDEC + 4 * GDN_HEADS]


def _halo_specs(tm, seq, width, rows):
    per = tm // rows
    last = seq // rows - 1
    prev = pl.BlockSpec((1, rows, width), lambda b, j: (b, jnp.maximum(j * per - 1, 0), 0))
    nxt = pl.BlockSpec((1, rows, width), lambda b, j: (b, jnp.minimum((j + 1) * per, last), 0))
    return prev, nxt


def _const_spec(shape):
    return pl.BlockSpec(shape, lambda b, j: (0,) * len(shape))


def _inproj(x, nw, wa, wg, wc, ws, gw, gb, conv, alog, dtb, tm):
    bsz, seq, _ = x.shape
    prev, nxt = _halo_specs(tm, seq, D_MODEL, HALO)
    tok = lambda w: pl.BlockSpec((1, tm, w), lambda b, j: (b, j, 0))
    out_shape = (
        jax.ShapeDtypeStruct((bsz, seq, GLA_QK * 2 + GLA_V), BF16),
        jax.ShapeDtypeStruct((bsz, seq, GLA_QK), F32),
        jax.ShapeDtypeStruct((bsz, seq, GLA_QK), F32),
        jax.ShapeDtypeStruct((bsz, seq, GDN_CONV_W), BF16),
        jax.ShapeDtypeStruct((bsz, seq, GLA_V + GDN_V), BF16),
        jax.ShapeDtypeStruct((bsz, seq, SM_W), F32),
        jax.ShapeDtypeStruct((bsz, 4 * GDN_HEADS, seq), F32),
    )
    out_specs = (tok(GLA_QK * 2 + GLA_V), tok(GLA_QK), tok(GLA_QK), tok(GDN_CONV_W),
                 tok(GLA_V + GDN_V), tok(SM_W),
                 pl.BlockSpec((1, 4 * GDN_HEADS, tm), lambda b, j: (b, 0, j)))
    in_specs = [tok(D_MODEL), prev, nxt] + [_const_spec(a.shape) for a in
                                            (nw, wa, wg, wc, ws, gw, gb, conv, alog, dtb)]
    return pl.pallas_call(
        _inproj_kernel,
        grid=(bsz, seq // tm),
        in_specs=in_specs,
        out_specs=out_specs,
        out_shape=out_shape,
        compiler_params=pltpu.CompilerParams(
            dimension_semantics=("arbitrary", "arbitrary"), vmem_limit_bytes=VMEM_LIMIT),
        name="inproj",
    )(x, x, x, nw, wa, wg, wc, ws, gw, gb, conv, alog, dtb)


def _mm(a, b):
    return jnp.dot(a, b, preferred_element_type=F32)


def _inv_unit_triangular(a_list, bd_mask, eye, lane_lo):
    def bd(y):
        return jnp.concatenate([jnp.where(lane_lo, y, 0.0).astype(BF16),
                                jnp.where(lane_lo, 0.0, y).astype(BF16)], axis=0)

    a_bd = [jnp.where(bd_mask, a, 0.0) for a in a_list]
    a_off = [bd(a - b) for a, b in zip(a_list, a_bd)]
    d = [eye - b for b in a_bd]
    x1 = [-b for b in a_bd]
    x2 = [_mm(v.astype(BF16), bd(v)) for v in x1]
    both = [_mm(jnp.concatenate([v, di], axis=0).astype(BF16), bd(v)) for v, di in zip(x2, d)]
    x4 = [b[:CHUNK] for b in both]
    d = [di + b[CHUNK:] for di, b in zip(d, both)]
    both = [_mm(jnp.concatenate([v, di], axis=0).astype(BF16), bd(v)) for v, di in zip(x4, d)]
    x8 = [b[:CHUNK] for b in both]
    d = [di + b[CHUNK:] for di, b in zip(d, both)]
    d = [di + _mm(di.astype(BF16), bd(v)) for di, v in zip(d, x8)]
    n = [_mm(di.astype(BF16), ao) for di, ao in zip(d, a_off)]
    n2 = [_mm(v.astype(BF16), bd(v)) for v in n]
    q = [eye - v for v in n]
    q = [qi + _mm(qi.astype(BF16), bd(v)) for qi, v in zip(q, n2)]
    return [_mm(qi.astype(BF16), bd(di)) for qi, di in zip(q, d)]


def _scan_kernel(qkv_f, lf_f, gdn_f, sm_f, smt_f, qkv_b, lf_b, gdn_b, sm_b, smt_b,
                 o_f, o_b, s_gla, s_gdn):
    @pl.when(pl.program_id(1) == 0)
    def _():
        s_gla[...] = jnp.zeros_like(s_gla)
        s_gdn[...] = jnp.zeros_like(s_gdn)

    n_chunks = qkv_f.shape[1] // CHUNK
    ri = lax.broadcasted_iota(jnp.int32, (CHUNK, CHUNK), 0)
    ci = lax.broadcasted_iota(jnp.int32, (CHUNK, CHUNK), 1)
    lower = ri >= ci
    upper = ri <= ci
    ri2 = lax.broadcasted_iota(jnp.int32, (CHUNK, 2 * CHUNK), 0)
    lane2 = lax.broadcasted_iota(jnp.int32, (CHUNK, 2 * CHUNK), 1)
    lane_lo = lane2 < CHUNK
    ci2 = jnp.where(lane_lo, lane2, lane2 - CHUNK)
    eye2 = jnp.where(ri2 == ci2, 1.0, 0.0).astype(F32)
    bd_mask2 = (ri2 // SUB) == (ci2 // SUB)
    tri_lo = jnp.where(lower, 1.0, 0.0).astype(BF16)
    tri_up = jnp.where(upper, 1.0, 0.0).astype(BF16)
    ones_bc = jnp.ones((CHUNK, GLA_DV), BF16)
    tn = (((0,), (0,)), ((), ()))

    refs = ((qkv_f, lf_f, gdn_f, sm_f, smt_f, o_f), (qkv_b, lf_b, gdn_b, sm_b, smt_b, o_b))
    consts = ((tri_lo, tri_up, lower, (ri2 >= ci2, ri2 > ci2), CHUNK - 1),
              (tri_up, tri_lo, upper, (ri2 <= ci2, ri2 < ci2), 0))

    units = []
    for step in range(n_chunks):
        for d in range(2):
            qkv, lf, gdn, sm, smt, _ = refs[d]
            tri, tri_t, _, _, _ = consts[d]
            c = step if d == 0 else n_chunks - 1 - step
            rows = slice(c * CHUNK, (c + 1) * CHUNK)
            x = qkv[0, rows, :].astype(F32)
            lf_hi, lf_lo = _split(lf[0, rows, :])
            small = sm[0, rows, :]
            sm_hi, sm_lo = _split(small)
            smt_hi, smt_lo = _split(smt[0, :, rows])
            units.append(dict(
                d=d, rows=rows, q=x[:, :GLA_QK], k=x[:, GLA_QK:2 * GLA_QK], v=x[:, 2 * GLA_QK:],
                y=gdn[0, rows, :].astype(F32), small=small,
                cum=_mm(tri, lf_hi) + _mm(tri, lf_lo),
                total=(lax.dot_general(lf_hi, ones_bc, tn, preferred_element_type=F32)
                       + lax.dot_general(lf_lo, ones_bc, tn, preferred_element_type=F32)),
                g_cols=_mm(tri, sm_hi) + _mm(tri, sm_lo),
                g_rows=_mm(smt_hi, tri_t) + _mm(smt_lo, tri_t)))

    first_of_pair = lax.broadcasted_iota(jnp.int32, (CHUNK, 2 * GLA_DK), 1) < GLA_DK
    half_row = lax.broadcasted_iota(jnp.int32, (CHUNK, GLA_QK), 0)
    for u in units:
        _, _, m_incl, _, last = consts[u["d"]]
        cum = u["cum"]
        q_in = u["q"] * jnp.exp(cum)
        k_out = u["k"] * jnp.exp(cum[last:last + 1] - cum)
        if u["d"] == 0:
            mid = cum[CHUNK // 2 - 1:CHUNK // 2]
            near = half_row < CHUNK // 2
        else:
            mid = cum[CHUNK // 2:CHUNK // 2 + 1]
            near = half_row >= CHUNK // 2
        q_att = [jnp.where(near, q_in, 0.0), jnp.where(near, 0.0, u["q"] * jnp.exp(cum - mid))]
        k_att = [jnp.where(near, u["k"] * jnp.exp(-cum), 0.0), u["k"] * jnp.exp(mid - cum)]
        q_sel, att, kv = [], [], []
        for pair in range(GLA_HEADS // 2):
            ps = slice(pair * 2 * GLA_DK, (pair + 1) * 2 * GLA_DK)
            vs = slice(pair * 2 * GLA_DV, (pair + 1) * 2 * GLA_DV)
            q_pair = [jnp.where(first_of_pair, q_in[:, ps], 0.0), jnp.where(first_of_pair, 0.0, q_in[:, ps])]
            k_pair = [jnp.where(first_of_pair, k_out[:, ps], 0.0), jnp.where(first_of_pair, 0.0, k_out[:, ps])]
            qa = [jnp.concatenate([jnp.where(first_of_pair, qr[:, ps], 0.0) for qr in q_att], axis=1),
                  jnp.concatenate([jnp.where(first_of_pair, 0.0, qr[:, ps]) for qr in q_att], axis=1)]
            scores = _dot_nt(jnp.concatenate(qa, axis=0),
                             jnp.concatenate([kr[:, ps] for kr in k_att], axis=1))
            q_sel += q_pair
            att += [jnp.where(m_incl, scores[:CHUNK], 0.0), jnp.where(m_incl, scores[CHUNK:], 0.0)]
            v_pair = jnp.concatenate([u["v"][:, vs][:, :GLA_DV], u["v"][:, vs][:, GLA_DV:]], axis=0)
            kv.append(_dot_tn(jnp.concatenate(k_pair, axis=0), v_pair))
        u["q_sel"] = q_sel
        u["att"] = att
        u["kv"] = jnp.concatenate(kv, axis=0)

    pairs = []
    zero = jnp.zeros((CHUNK, GDN_DK), F32)
    zero2 = jnp.zeros((CHUNK, 2 * GDN_DV), F32)
    for u in units:
        d = u["d"]
        _, _, _, (m_incl2, m_strict2), last = consts[d]
        y = u["y"]
        for pr in range(GDN_HEADS // 2):
            hs = (2 * pr, 2 * pr + 1)
            q = [y[:, h * GDN_DK:(h + 1) * GDN_DK] for h in hs]
            k = [y[:, GDN_QK + h * GDN_DK:GDN_QK + (h + 1) * GDN_DK] for h in hs]
            v = [y[:, 2 * GDN_QK + h * GDN_DV:2 * GDN_QK + (h + 1) * GDN_DV] for h in hs]
            g_bc = [jnp.broadcast_to(u["g_cols"][:, SM_GDEC + GDN_HEADS * d + h:SM_GDEC + GDN_HEADS * d + h + 1],
                                     (CHUNK, GDN_DK)) for h in hs]
            b_bc = [jnp.broadcast_to(u["small"][:, SM_BETA + GDN_HEADS * d + h:SM_BETA + GDN_HEADS * d + h + 1],
                                     (CHUNK, GDN_DK)) for h in hs]
            g_row = [u["g_rows"][GDN_HEADS * d + h:GDN_HEADS * d + h + 1, :] for h in hs]
            g2 = jnp.where(lane_lo, g_bc[0], g_bc[1])
            b2 = jnp.where(lane_lo, b_bc[0], b_bc[1])
            decay2 = jnp.where(m_incl2, jnp.exp(g2 - jnp.concatenate(g_row, axis=1)), 0.0)
            e_g = [jnp.exp(g) for g in g_bc]
            g_last = [g[:, last:last + 1] for g in g_row]
            lhs = jnp.concatenate([jnp.concatenate([q[0], k[0]], axis=0),
                                   jnp.concatenate([q[1], k[1]], axis=0)], axis=1)
            k_bd = jnp.concatenate([jnp.concatenate([k[0], zero], axis=1),
                                    jnp.concatenate([zero, k[1]], axis=1)], axis=0)
            p2 = _dot_nt(lhs, k_bd)
            rhs = [jnp.concatenate([v[i] * b_bc[i], k[i] * (b_bc[i] * e_g[i])], axis=1) for i in range(2)]
            pairs.append(dict(
                u=u, hs=hs, qk=p2[:CHUNK] * decay2,
                a=jnp.where(m_strict2, p2[CHUNK:] * b2 * decay2, 0.0),
                rhs=jnp.concatenate([jnp.concatenate([rhs[0], zero2], axis=1),
                                     jnp.concatenate([zero2, rhs[1]], axis=1)], axis=0),
                q_dec=[q[i] * e_g[i] for i in range(2)],
                k_dec=[k[i] * jnp.exp(g_last[i] - g_bc[i]) for i in range(2)],
                e_last=[jnp.exp(g) for g in g_last]))
    t_list = _inv_unit_triangular([c["a"] for c in pairs], bd_mask2, eye2, lane_lo)
    for c, t in zip(pairs, t_list):
        c["uw"] = _dot(t, c["rhs"])

    sg = [s_gla[0], s_gla[1]]
    sd = [[s_gdn[d, h] for h in range(GDN_HEADS)] for d in range(2)]
    per_step = 2 * (GDN_HEADS // 2)
    width = 2 * GDN_DV
    for step in range(n_chunks):
        for d in range(2):
            u = units[2 * step + d]
            o_ref = refs[d][5]
            s = sg[d]
            for h in range(GLA_HEADS):
                ps = slice((h // 2) * 2 * GLA_DK, (h // 2 + 1) * 2 * GLA_DK)
                vs = slice(h * GLA_DV, (h + 1) * GLA_DV)
                lhs = jnp.concatenate([u["q_sel"][h], u["att"][h]], axis=1)
                rhs = jnp.concatenate([s[ps], u["v"][:, vs]], axis=0)
                o_ref[0, u["rows"], vs] = _dot(lhs, rhs)
            sg[d] = jnp.exp(u["total"]) * s + u["kv"]
        step_pairs = pairs[per_step * step:per_step * (step + 1)]
        cs = [(c, i) for c in step_pairs for i in range(2)]
        r = [_dot(jnp.concatenate([c["uw"][:, i * width + GDN_DV:(i + 1) * width], c["q_dec"][i]], axis=0),
                  sd[c["u"]["d"]][c["hs"][i]]) for c, i in cs]
        v_new = [c["uw"][:, i * width:i * width + GDN_DV] - ri_[:CHUNK] for (c, i), ri_ in zip(cs, r)]
        for n_, c in enumerate(step_pairs):
            d = c["u"]["d"]
            o_ref = refs[d][5]
            qk_sel = jnp.concatenate([jnp.where(lane_lo, c["qk"], 0.0), jnp.where(lane_lo, 0.0, c["qk"])], axis=0)
            intra = _dot(qk_sel, jnp.concatenate(v_new[2 * n_:2 * n_ + 2], axis=0))
            for i in range(2):
                h = c["hs"][i]
                o_ref[0, c["u"]["rows"], GLA_V + h * GDN_DV:GLA_V + (h + 1) * GDN_DV] = (
                    r[2 * n_ + i][CHUNK:] + intra[i * CHUNK:(i + 1) * CHUNK])
                sd[d][h] = c["e_last"][i] * sd[d][h] + _dot_tn(c["k_dec"][i], v_new[2 * n_ + i])
    for d in range(2):
        s_gla[d] = sg[d]
        for h in range(GDN_HEADS):
            s_gdn[d, h] = sd[d][h]


def _scan(qkv, lff, lfb, gdn, small, small_t, tb):
    bsz, seq, _ = qkv.shape
    nb = seq // tb
    fwd = lambda w: pl.BlockSpec((1, tb, w), lambda b, j: (b, j, 0))
    bwd = lambda w: pl.BlockSpec((1, tb, w), lambda b, j: (b, nb - 1 - j, 0))
    rows = 4 * GDN_HEADS
    in_specs = [fwd(qkv.shape[2]), fwd(GLA_QK), fwd(GDN_CONV_W), fwd(SM_W),
                pl.BlockSpec((1, rows, tb), lambda b, j: (b, 0, j)),
                bwd(qkv.shape[2]), bwd(GLA_QK), bwd(GDN_CONV_W), bwd(SM_W),
                pl.BlockSpec((1, rows, tb), lambda b, j: (b, 0, nb - 1 - j))]
    width = GLA_V + GDN_V
    return pl.pallas_call(
        _scan_kernel,
        grid=(bsz, nb),
        in_specs=in_specs,
        out_specs=(fwd(width), bwd(width)),
        out_shape=(jax.ShapeDtypeStruct((bsz, seq, width), F32),) * 2,
        scratch_shapes=[pltpu.VMEM((2, GLA_QK, GLA_DV), F32),
                        pltpu.VMEM((2, GDN_HEADS, GDN_DK, GDN_DV), F32)],
        compiler_params=pltpu.CompilerParams(
            dimension_semantics=("arbitrary", "arbitrary"), vmem_limit_bytes=VMEM_LIMIT),
        name="scan",
    )(qkv, lff, gdn, small, small_t, qkv, lfb, gdn, small, small_t)


def _mix_ffn_kernel(of_ref, ofp_ref, ofn_ref, ob_ref, obp_ref, obn_ref, g_ref, gp_ref, gx_ref,
                    x_ref, xp_ref, xn_ref, gn_ref, wout_ref, nw_ref, wup_ref, conv_ref, wdown_ref,
                    fw_ref, out_ref):
    j = pl.program_id(1)
    nj = pl.num_programs(1)
    tm = x_ref.shape[1]
    half = tm // 2
    ext = half + 2 * HALO
    n_tiles = D_FF // FF_TILE
    gn = gn_ref[...]
    row = lax.broadcasted_iota(jnp.int32, (ext, D_MODEL), 0)
    pad_row = ((row == HALO - 1) & (j == 0), (row == HALO + half) & (j == nj - 1))

    def frame(idx, main_ref, prev, nxt):
        if idx == 0:
            return jnp.concatenate([prev, main_ref[0, :half + HALO, :]], axis=0)
        return jnp.concatenate([main_ref[0, half - HALO:, :], nxt], axis=0)

    def gate_frame(idx):
        if idx == 0:
            main = g_ref[0, :half + 2 * HALO, :].astype(F32)[:half + HALO]
            return jnp.concatenate([gp_ref[0].astype(F32)[HALO:], main], axis=0)
        main = g_ref[0, half - 2 * HALO:, :].astype(F32)[HALO:]
        return jnp.concatenate([main, gx_ref[0].astype(F32)[:HALO]], axis=0)

    state = {}

    def prologue(idx):
        o = (frame(idx, of_ref, ofp_ref[0], ofn_ref[0])
             + frame(idx, ob_ref, obp_ref[0], obn_ref[0]))
        g = gate_frame(idx)
        parts = []
        for h in range(GLA_HEADS + GDN_HEADS):
            hs = slice(h * GLA_DV, (h + 1) * GLA_DV)
            gate = g[:, hs]
            parts.append(_rms(o[:, hs], gn[:, hs]) * (gate * _sigmoid(gate)))
        y = jnp.concatenate(parts, axis=-1).astype(BF16)
        x1 = (frame(idx, x_ref, xp_ref[0], xn_ref[0])
              + jnp.dot(y, wout_ref[...], preferred_element_type=F32))
        he = jnp.where(pad_row[idx], 0.0, _rms(x1, nw_ref[...])).astype(BF16)
        state[idx] = (x1[HALO:HALO + half], he)

    def up_all(idx):
        _, he = state[idx]
        acts = []
        for t in range(n_tiles):
            cols = [slice(part * D_FF + t * FF_TILE, part * D_FF + (t + 1) * FF_TILE) for part in range(2)]
            val, gate = [(pltpu.roll(e, 1, 0) * w[0:1] + e * w[1:2] + pltpu.roll(e, ext - 1, 0) * w[2:3])
                         [HALO:HALO + half]
                         for e, w in ((jnp.dot(he, wup_ref[:, c], preferred_element_type=F32), conv_ref[:, c])
                                      for c in cols)]
            acts.append((gate * _sigmoid(gate) * val).astype(BF16))
        return jnp.concatenate(acts, axis=1)

    def down(idx, act):
        x1, _ = state[idx]
        acc = jnp.dot(act, wdown_ref[...], preferred_element_type=F32)
        out_ref[0, idx * half:(idx + 1) * half, :] = _rms(x1 + acc, fw_ref[...])

    prologue(0)
    prologue(1)
    act0 = up_all(0)
    act1 = up_all(1)
    down(0, act0)
    down(1, act1)


def _mix_ffn(o_f, o_b, gates, x, gn, wout, nw, wup, conv, wdown, fw, tm):
    bsz, seq, _ = x.shape
    prev, nxt = _halo_specs(tm, seq, D_MODEL, HALO)
    prev16, nxt16 = _halo_specs(tm, seq, D_MODEL, 2 * HALO)
    tok = pl.BlockSpec((1, tm, D_MODEL), lambda b, j: (b, j, 0))
    once = lambda a: pl.BlockSpec(a.shape, lambda b, j: (0,) * a.ndim, pipeline_mode=pl.Buffered(1))
    in_specs = ([tok, prev, nxt, tok, prev, nxt, tok, prev16, nxt16, tok, prev, nxt]
                + [_const_spec(gn.shape), once(wout), _const_spec(nw.shape), once(wup),
                   _const_spec(conv.shape), once(wdown), _const_spec(fw.shape)])
    return pl.pallas_call(
        _mix_ffn_kernel,
        grid=(bsz, seq // tm),
        in_specs=in_specs,
        out_specs=tok,
        out_shape=jax.ShapeDtypeStruct((bsz, seq, D_MODEL), F32),
        compiler_params=pltpu.CompilerParams(
            dimension_semantics=("arbitrary", "arbitrary"), vmem_limit_bytes=VMEM_LIMIT),
        name="mix_ffn",
    )(o_f, o_f, o_f, o_b, o_b, o_b, gates, gates, gates, x, x, x, gn, wout, nw, wup, conv, wdown, fw)


def _stage_params(attn_norm, w_in, gla_gate_w, gla_gate_b, gla_norm, gdn_conv, gdn_a_log, gdn_dt_bias,
                  gdn_norm, w_out, ffn_norm, w_up, ffn_conv, w_down, final_norm):
    w = w_in[0]
    o_gg = 2 * GLA_QK + GLA_V
    o_lr = o_gg + GLA_V
    o_dq = o_lr + 2 * GLA_RANK
    o_dg = o_dq + GDN_CONV_W
    o_da = o_dg + GDN_V
    wa = w[:, :o_gg].astype(BF16)
    wg = jnp.concatenate([w[:, o_gg:o_lr], w[:, o_dg:o_da]], axis=1).astype(BF16)
    wc = w[:, o_dq:o_dg].astype(BF16)
    ws = jnp.concatenate([w[:, o_lr:o_dq], w[:, o_da:],
                          jnp.zeros((D_MODEL, SM_W - 2 * GLA_RANK - 4 * GDN_HEADS), F32)], axis=1).astype(BF16)
    gw = jnp.zeros((SM_W, 2 * GLA_QK), F32)
    gw = gw.at[:GLA_RANK, :GLA_QK].set(gla_gate_w[0, 0])
    gw = gw.at[GLA_RANK:2 * GLA_RANK, GLA_QK:].set(gla_gate_w[0, 1]).astype(BF16)
    gb = gla_gate_b[0].reshape(1, 2 * GLA_QK)
    pad8 = lambda a: jnp.concatenate([a, jnp.zeros((HALO - a.shape[0],) + a.shape[1:], a.dtype)], axis=0)
    conv = pad8(gdn_conv[0])
    lanes = jnp.zeros((1, SM_W), F32)
    alog = lanes.at[0, SM_GDEC:SM_BETA].set(gdn_a_log[0].reshape(-1))
    dtb = lanes.at[0, SM_GDEC:SM_BETA].set(gdn_dt_bias[0].reshape(-1))
    gn = jnp.concatenate([jnp.tile(gla_norm[0], GLA_HEADS), jnp.tile(gdn_norm[0], GDN_HEADS)]).reshape(1, -1)
    return dict(
        attn_norm=attn_norm[0].reshape(1, -1), wa=wa, wg=wg, wc=wc, ws=ws, gw=gw, gb=gb, conv=conv,
        alog=alog, dtb=dtb, gn=gn, wout=w_out[0].astype(BF16), ffn_norm=ffn_norm[0].reshape(1, -1),
        wup=w_up[0].astype(BF16), fconv=pad8(ffn_conv[0]), wdown=w_down[0].astype(BF16),
        final_norm=final_norm.reshape(1, -1))


def _tile(seq, want):
    t = min(want, seq)
    assert seq % t == 0 and t % 16 == 0
    return t


def _encode(x, p):
    seq = x.shape[1]
    tm = _tile(seq, 512)
    qkv, lff, lfb, gdn, gates, small, small_t = _inproj(
        x, p["attn_norm"], p["wa"], p["wg"], p["wc"], p["ws"], p["gw"], p["gb"], p["conv"],
        p["alog"], p["dtb"], tm)
    o_f, o_b = _scan(qkv, lff, lfb, gdn, small, small_t, tm)
    return _mix_ffn(o_f, o_b, gates, x, p["gn"], p["wout"], p["ffn_norm"], p["wup"], p["fconv"], p["wdown"],
                    p["final_norm"], tm)


def kernel(x_prompt, x_sample, attn_norm, w_in, gla_gate_w, gla_gate_b, gla_norm, gdn_conv, gdn_a_log,
           gdn_dt_bias, gdn_norm, w_out, ffn_norm, w_up, ffn_conv, w_down, final_norm):
    p = _stage_params(attn_norm, w_in, gla_gate_w, gla_gate_b, gla_norm, gdn_conv, gdn_a_log, gdn_dt_bias,
                      gdn_norm, w_out, ffn_norm, w_up, ffn_conv, w_down, final_norm)
    return (_encode(x_prompt, p), _encode(x_sample, p))
```
